```python
import jax, jax.numpy as jnp
from jax import lax
import numpy as np

D_MODEL = 1024
BATCH = 8
SEQ = 8192
DEPTH = 1

N_META = 16
BLOCK = 128
WINDOW = 128
HEAD_DIM = 64
ATT_Q_HEADS = D_MODEL // HEAD_DIM
ATT_KV_HEADS = max(ATT_Q_HEADS // 8, 1)
ATT_GROUP = ATT_Q_HEADS // ATT_KV_HEADS
ATT_WIDTH = ATT_Q_HEADS * HEAD_DIM
ATT_KV_WIDTH = ATT_KV_HEADS * HEAD_DIM
ROPE_THETA = 10000.0
RWKV_HEAD = 64
RWKV_HEADS = D_MODEL // RWKV_HEAD
RWKV_WIDTH = RWKV_HEADS * RWKV_HEAD
DECAY_LORA = 64
ICLR_LORA = 64
RWKV_SHIFT_WIDTH = 3 * RWKV_WIDTH + DECAY_LORA + ICLR_LORA
RMS_EPS = 1e-6
GN_EPS = 64e-5
NEG_INF = -1e30
SPLIT_SIZES = (ATT_WIDTH, ATT_KV_WIDTH, ATT_KV_WIDTH, ATT_WIDTH, RWKV_WIDTH, RWKV_WIDTH, RWKV_WIDTH, DECAY_LORA, ICLR_LORA, RWKV_WIDTH, D_MODEL, D_MODEL)
IN_WIDTH = 2 * ATT_WIDTH + 2 * ATT_KV_WIDTH + 4 * RWKV_WIDTH + DECAY_LORA + ICLR_LORA + 2 * D_MODEL

kernel_name = 'hybrid_swa_sink_rwkv7_gated_merge'


def _offsets(sizes):
    out, acc = [], 0
    for s in sizes[:-1]:
        acc += s
        out.append(acc)
    return out


def _rmsnorm(x, w):
    xf = x.astype(jnp.float32)
    y = xf * lax.rsqrt(jnp.mean(xf * xf, axis=-1, keepdims=True) + RMS_EPS)
    return (y * w.astype(jnp.float32)).astype(x.dtype)


def _rope(x, pos):
    half = x.shape[-1] // 2
    inv = 1.0 / (ROPE_THETA ** (jnp.arange(half, dtype=jnp.float32) / half))
    ang = pos[:, None] * inv[None, :]
    cos = jnp.cos(ang)[:, None, :]
    sin = jnp.sin(ang)[:, None, :]
    xf = x.astype(jnp.float32)
    x1, x2 = xf[..., :half], xf[..., half:]
    return jnp.concatenate([x1 * cos - x2 * sin, x2 * cos + x1 * sin], axis=-1).astype(x.dtype)


def _token_shift(z):
    return jnp.pad(z, ((0, 0), (1, 0), (0, 0)))[:, :-1]


def _sliding_window_gqa(q, k, v, sinks):
    B, T = q.shape[0], q.shape[1]
    pad = (-T) % BLOCK
    Tp = T + pad
    nb = Tp // BLOCK
    padw = ((0, 0), (pad, 0), (0, 0), (0, 0))
    q = jnp.pad(q, padw).reshape(B, nb, BLOCK, ATT_KV_HEADS, ATT_GROUP, HEAD_DIM)
    k = jnp.pad(k, padw).reshape(B, nb, BLOCK, ATT_KV_HEADS, HEAD_DIM)
    v = jnp.pad(v, padw).reshape(B, nb, BLOCK, ATT_KV_HEADS, HEAD_DIM)

    def window(t):
        prev = jnp.concatenate([jnp.zeros_like(t[:, :1]), t[:, :-1]], axis=1)
        return jnp.moveaxis(jnp.concatenate([prev, t], axis=2), 1, 0)

    kw, vw = window(k), window(v)
    qm = jnp.moveaxis(q, 1, 0)
    sink = sinks.astype(jnp.float32).reshape(ATT_KV_HEADS, ATT_GROUP)[None, :, :, None, None]
    scale = HEAD_DIM ** -0.5

    def block(args):
        n, qn, kn, vn = args
        s = jnp.einsum('bqhgd,bkhd->bhgqk', qn, kn).astype(jnp.float32) * scale
        qi = n * BLOCK + jnp.arange(BLOCK)
        kj = (n - 1) * BLOCK + jnp.arange(2 * BLOCK)
        diff = qi[:, None] - kj[None, :]
        ok = (diff >= 0) & (diff < WINDOW) & (kj[None, :] >= pad)
        s = jnp.where(ok, s, NEG_INF)
        sb = jnp.broadcast_to(sink, s.shape[:-1] + (1,))
        p = jax.nn.softmax(jnp.concatenate([s, sb], axis=-1), axis=-1)[..., :-1]
        return jnp.einsum('bhgqk,bkhd->bqhgd', p.astype(vn.dtype), vn)

    o = lax.map(block, (jnp.arange(nb), qm, kw, vw))
    o = jnp.moveaxis(o, 0, 1).reshape(B, Tp, ATT_WIDTH)
    return o[:, pad:]


def _rwkv7_time_mix(r, k, v, w_lo, a_lo, mu, w0, w2, a0, a2, k_k, k_a, r_k, ln_w, ln_b):
    f32 = jnp.float32
    B, T = r.shape[0], r.shape[1]
    z = jnp.concatenate([r, k, v, w_lo, a_lo], axis=-1).astype(f32)
    z = z + (_token_shift(z) - z) * mu.astype(f32)
    r, k, v, w_lo, a_lo = jnp.split(z, [RWKV_WIDTH, 2 * RWKV_WIDTH, 3 * RWKV_WIDTH, 3 * RWKV_WIDTH + DECAY_LORA], axis=-1)
    w = -jax.nn.softplus(-(w0.astype(f32) + jnp.tanh(w_lo) @ w2.astype(f32))) - 0.5
    decay = jnp.exp(-jnp.exp(w))
    a = jax.nn.sigmoid(a0.astype(f32) + a_lo @ a2.astype(f32))

    def hs(t):
        return t.reshape(B, T, RWKV_HEADS, RWKV_HEAD)

    kk = hs(k * k_k.astype(f32))
    kk = kk / jnp.maximum(jnp.sqrt(jnp.sum(kk * kk, axis=-1, keepdims=True)), 1e-12)
    k = k * (1.0 + (a - 1.0) * k_a.astype(f32))
    r, k, v, decay, a = hs(r), hs(k), hs(v), hs(decay), hs(a)
    xs = tuple(jnp.moveaxis(t, 1, 0) for t in (r, decay, k, v, -kk, kk * a))

    def step(S, inp):
        r_t, w_t, k_t, v_t, a_t, b_t = inp
        sa = jnp.einsum('bhij,bhj->bhi', S, a_t)
        S = S * w_t[:, :, None, :] + sa[..., None] * b_t[:, :, None, :] + v_t[..., None] * k_t[:, :, None, :]
        return S, jnp.einsum('bhij,bhj->bhi', S, r_t)

    S0 = jnp.zeros((B, RWKV_HEADS, RWKV_HEAD, RWKV_HEAD), f32)
    _, y = lax.scan(step, S0, xs)
    y = jnp.moveaxis(y, 0, 1)
    mean = jnp.mean(y, axis=-1, keepdims=True)
    var = jnp.mean(jnp.square(y - mean), axis=-1, keepdims=True)
    y = ((y - mean) * lax.rsqrt(var + GN_EPS)).reshape(B, T, RWKV_WIDTH) * ln_w.astype(f32) + ln_b.astype(f32)
    bonus = jnp.sum(r * k * r_k.astype(f32), axis=-1, keepdims=True) * v
    return y + bonus.reshape(B, T, RWKV_WIDTH)


def setup_inputs(seed: int = 0) -> dict:
    key = jax.random.key(seed)
    ks = jax.random.split(key, 20)
    f32 = jnp.float32

    def nrm(k, shape, s):
        return jax.random.normal(k, shape, f32) * s

    L = DEPTH
    return {
        'x': nrm(ks[0], (BATCH, SEQ, D_MODEL), 1.0),
        'meta_tokens': nrm(ks[1], (N_META, D_MODEL), 1.0),
        'norm_w': 1.0 + nrm(ks[2], (L, D_MODEL), 0.05),
        'w_in': nrm(ks[3], (L, D_MODEL, IN_WIDTH), D_MODEL ** -0.5),
        'att_sinks': nrm(ks[4], (L, ATT_Q_HEADS), 1.0),
        'rk_mu': jax.random.uniform(ks[5], (L, RWKV_SHIFT_WIDTH), f32, 0.0, 1.0),
        'rk_w0': jax.random.uniform(ks[6], (L, RWKV_WIDTH), f32, -6.0, -1.0),
        'rk_w2': nrm(ks[7], (L, DECAY_LORA, RWKV_WIDTH), 0.1),
        'rk_a0': nrm(ks[8], (L, RWKV_WIDTH), 0.5),
        'rk_a2': nrm(ks[9], (L, ICLR_LORA, RWKV_WIDTH), 0.5 * ICLR_LORA ** -0.5),
        'rk_k_k': 0.85 + nrm(ks[10], (L, RWKV_WIDTH), 0.05),
        'rk_k_a': 1.0 + nrm(ks[11], (L, RWKV_WIDTH), 0.05),
        'rk_r_k': nrm(ks[12], (L, RWKV_HEADS, RWKV_HEAD), 0.1),
        'rk_ln_w': 1.0 + nrm(ks[13], (L, RWKV_WIDTH), 0.05),
        'rk_ln_b': nrm(ks[14], (L, RWKV_WIDTH), 0.01),
        'w_branch_att': nrm(ks[15], (L, ATT_WIDTH, D_MODEL), ATT_WIDTH ** -0.5),
        'w_branch_rwkv': nrm(ks[16], (L, RWKV_WIDTH, D_MODEL), RWKV_WIDTH ** -0.5),
        'w_out': nrm(ks[17], (L, D_MODEL, D_MODEL), D_MODEL ** -0.5),
        'final_norm_w': 1.0 + nrm(ks[18], (D_MODEL,), 0.05),
    }


def reference(x, meta_tokens, norm_w, w_in, att_sinks, rk_mu, rk_w0, rk_w2, rk_a0, rk_a2, rk_k_k, rk_k_a, rk_r_k, rk_ln_w, rk_ln_b, w_branch_att, w_branch_rwkv, w_out, final_norm_w):
    B = x.shape[0]
    meta = jnp.broadcast_to(meta_tokens.astype(x.dtype)[None], (B, N_META, D_MODEL))
    h = jnp.concatenate([meta, x], axis=1)
    T = h.shape[1]
    pos = jnp.arange(T, dtype=jnp.float32)
    split_at = _offsets(SPLIT_SIZES)
    for l in range(DEPTH):
        u = _rmsnorm(h, norm_w[l])
        p = u @ w_in[l].astype(u.dtype)
        q, ka, va, ga, r, kr, vr, wl, al, gr, ma, mr = jnp.split(p, split_at, axis=-1)
        q = _rope(q.reshape(B, T, ATT_Q_HEADS, HEAD_DIM), pos)
        ka = _rope(ka.reshape(B, T, ATT_KV_HEADS, HEAD_DIM), pos)
        va = va.reshape(B, T, ATT_KV_HEADS, HEAD_DIM)
        att = _sliding_window_gqa(q, ka, va, att_sinks[l])
        rw = _rwkv7_time_mix(r, kr, vr, wl, al, rk_mu[l], rk_w0[l], rk_w2[l], rk_a0[l], rk_a2[l], rk_k_k[l], rk_k_a[l], rk_r_k[l], rk_ln_w[l], rk_ln_b[l]).astype(h.dtype)
        ya = (att * jax.nn.silu(ga)) @ w_branch_att[l].astype(h.dtype)
        yr = (rw * jax.nn.silu(gr)) @ w_branch_rwkv[l].astype(h.dtype)
        merged = jax.nn.sigmoid(ma) * ya + jax.nn.sigmoid(mr) * yr
        h = h + merged @ w_out[l].astype(h.dtype)
    y = _rmsnorm(h, final_norm_w)
    return y[:, N_META:]
```

```python
import functools

import jax
import jax.numpy as jnp
from jax import lax
from jax.experimental import pallas as pl
from jax.experimental.pallas import tpu as pltpu

F32 = jnp.float32
BF16 = jnp.bfloat16

N_META = 16
BLOCK = 128
HEAD = 64
KV_HEADS = 2
LORA = 64
CHUNK = 64
ROPE_THETA = 10000.0
RMS_EPS = 1e-6
GN_EPS = 64e-5
NEG_INF = -1e30
LANES = 128
VMEM_LIMIT = 48 * 1024 * 1024

NT_DIMS = (((1,), (1,)), ((), ()))
TN_DIMS = (((0,), (0,)), ((), ()))


def _dot(a, b):
    return jnp.dot(a, b, preferred_element_type=F32)


def _dot_nt(a, b):
    return lax.dot_general(a, b, NT_DIMS, preferred_element_type=F32)


def _dot_tn(a, b):
    return lax.dot_general(a, b, TN_DIMS, preferred_element_type=F32)


def _row_tile(m):
    for t in (1024, 512, 256, 128):
        if m % t == 0:
            return t
    raise ValueError(f"row count {m} must be a multiple of 128")


def _rope(acc, cs, reps):
    width = acc.shape[1]
    cos = cs[:, :LANES]
    sin = cs[:, LANES:]
    if reps > 1:
        cos = jnp.tile(cos, (1, reps))
        sin = jnp.tile(sin, (1, reps))
    lane = lax.broadcasted_iota(jnp.int32, acc.shape, 1)
    first_half = (lane & (HEAD - 1)) < (HEAD // 2)
    rot = jnp.where(first_half, pltpu.roll(acc, width - HEAD // 2, axis=1), pltpu.roll(acc, HEAD // 2, axis=1))
    return acc * cos + rot * sin


def _inproj_kernel(x_ref, nw_ref, cs_ref, wm_ref, wt_ref, pm_ref, pt_ref, xn_ref, *, n_main):
    j = pl.program_id(1)

    @pl.when(j == 0)
    def _():
        x = x_ref[...]
        ms = jnp.mean(x * x, axis=-1, keepdims=True)
        xn_ref[...] = (x * lax.rsqrt(ms + RMS_EPS) * nw_ref[...]).astype(BF16)
        acc = _dot(xn_ref[...], wm_ref[...])
        pm_ref[...] = _rope(acc, cs_ref[...], acc.shape[1] // LANES).astype(BF16)

    @pl.when(jnp.logical_and(j > 0, j < n_main))
    def _():
        pm_ref[...] = _dot(xn_ref[...], wm_ref[...]).astype(BF16)

    @pl.when(j == n_main)
    def _():
        acc = _dot(xn_ref[...], wt_ref[...])
        ka = _rope(acc[:, :LANES], cs_ref[...], 1)
        pt_ref[:, :LANES] = ka.astype(BF16)
        pt_ref[:, LANES:] = acc[:, LANES:].astype(BF16)


def _inproj(x2d, norm_w, cs, w_main, w_tail):
    m, d = x2d.shape
    tm = _row_tile(cs.shape[0])
    cs_period_tiles = cs.shape[0] // tm
    n_main = w_main.shape[1] // d
    tail = w_tail.shape[1]
    kern = functools.partial(_inproj_kernel, n_main=n_main)
    return pl.pallas_call(
        kern,
        grid=(m // tm, n_main + 1),
        in_specs=[
            pl.BlockSpec((tm, d), lambda i, j: (i, 0)),
            pl.BlockSpec((1, d), lambda i, j: (0, 0)),
            pl.BlockSpec((tm, 2 * LANES), lambda i, j: (i % cs_period_tiles, 0)),
            pl.BlockSpec((d, d), lambda i, j: (0, jnp.minimum(j, n_main - 1))),
            pl.BlockSpec((d, tail), lambda i, j: (0, 0)),
        ],
        out_specs=[
            pl.BlockSpec((tm, d), lambda i, j: (i, jnp.minimum(j, n_main - 1))),
            pl.BlockSpec((tm, tail), lambda i, j: (i, 0)),
        ],
        out_shape=[
            jax.ShapeDtypeStruct((m, n_main * d), BF16),
            jax.ShapeDtypeStruct((m, tail), BF16),
        ],
        scratch_shapes=[pltpu.VMEM((tm, d), BF16)],
        compiler_params=pltpu.CompilerParams(
            dimension_semantics=("parallel", "arbitrary"), vmem_limit_bytes=VMEM_LIMIT),
        name="inproj",
    )(x2d, norm_w, cs, w_main, w_tail)


def _attn_kernel(sink_ref, q_ref, kvp_ref, kvc_ref, kvm_ref, o_ref, *, n_pairs, group_pairs, scale):
    n = pl.program_id(1)
    prev = jnp.where(n == 0, kvm_ref[...], kvp_ref[...])
    kv = jnp.concatenate([prev, kvc_ref[...]], axis=0).astype(F32)
    k = kv[:, :LANES]
    v = kv[:, LANES:2 * LANES]
    lane_kv = lax.broadcasted_iota(jnp.int32, k.shape, 1)
    low = lane_kv < HEAD
    k_sw = pltpu.roll(k, HEAD, axis=1)
    v_sw = pltpu.roll(v, HEAD, axis=1)
    k_dup = [jnp.where(low, k, k_sw).astype(BF16), jnp.where(low, k_sw, k).astype(BF16)]
    v_dup = [jnp.where(low, v, v_sw).astype(BF16), jnp.where(low, v_sw, v).astype(BF16)]

    rows = 2 * BLOCK
    r = lax.broadcasted_iota(jnp.int32, (rows, rows), 0) & (BLOCK - 1)
    c = lax.broadcasted_iota(jnp.int32, (rows, rows), 1)
    first_real = jnp.where(n == 0, BLOCK - N_META, 0)
    ok = (c > r) & (c <= r + BLOCK) & (c >= first_real)
    top = lax.broadcasted_iota(jnp.int32, (rows, 1), 0) < BLOCK
    lane_q = lax.broadcasted_iota(jnp.int32, (BLOCK, LANES), 1)
    low_q = lane_q < HEAD

    for i in range(n_pairs):
        g = i // group_pairs
        q2 = q_ref[:, i * LANES:(i + 1) * LANES]
        zero = jnp.zeros_like(q2)
        qs = jnp.concatenate([jnp.where(low_q, q2, zero), jnp.where(low_q, zero, q2)], axis=0)
        s = _dot_nt(qs, k_dup[g]) * scale
        s = jnp.where(ok, s, NEG_INF)
        sink = jnp.where(top, sink_ref[2 * i], sink_ref[2 * i + 1])
        mx = jnp.maximum(jnp.max(s, axis=-1, keepdims=True), sink)
        e = jnp.exp(s - mx)
        den = jnp.sum(e, axis=-1, keepdims=True) + jnp.exp(sink - mx)
        o = _dot(e.astype(BF16), v_dup[g]) * (1.0 / den)
        o_ref[:, i * LANES:(i + 1) * LANES] = jnp.where(low_q, o[:BLOCK], o[BLOCK:]).astype(o_ref.dtype)


def _attention(sinks, p_main, p_tail, pt_meta, batch, seq):
    d = p_main.shape[1] // 8
    tail = p_tail.shape[1]
    nbq = seq // BLOCK
    n_pairs = d // LANES
    kern = functools.partial(_attn_kernel, n_pairs=n_pairs, group_pairs=n_pairs // KV_HEADS, scale=HEAD ** -0.5)
    return pl.pallas_call(
        kern,
        grid=(batch, nbq),
        in_specs=[
            pl.BlockSpec(memory_space=pltpu.SMEM),
            pl.BlockSpec((BLOCK, d), lambda b, n: (b * nbq + n, 0)),
            pl.BlockSpec((BLOCK, tail), lambda b, n: (b * nbq + jnp.maximum(n - 1, 0), 0)),
            pl.BlockSpec((BLOCK, tail), lambda b, n: (b * nbq + n, 0)),
            pl.BlockSpec((BLOCK, tail), lambda b, n: (0, 0)),
        ],
        out_specs=pl.BlockSpec((BLOCK, d), lambda b, n: (b * nbq + n, 0)),
        out_shape=jax.ShapeDtypeStruct((batch * seq, d), BF16),
        compiler_params=pltpu.CompilerParams(
            dimension_semantics=("parallel", "arbitrary"), vmem_limit_bytes=VMEM_LIMIT),
        name="swa_attention",
    )(sinks, p_main, p_tail, p_tail, pt_meta)


def _stack_heads(x, low):
    zero = jnp.zeros_like(x)
    return jnp.concatenate([jnp.where(low, x, zero), jnp.where(low, zero, x)], axis=0)


def _rwkv_pair(r, k2, v, lw, cum, av, bv, s_prev, masks):
    low, tri_strict, tri_incl, eye_cat, blockdiag = masks
    L = r.shape[0]
    p_incl = jnp.exp(cum)
    p_excl = jnp.exp(cum - lw)
    p_inv = jnp.exp(-cum)
    p_last = p_incl[L - 1:L, :]
    a_t = av * p_excl
    r_t = r * p_incl
    b_t = bv * p_inv
    k_t = k2 * p_inv

    lhs = jnp.concatenate([a_t, r_t], axis=0).astype(BF16)
    rhs = jnp.concatenate([_stack_heads(b_t, low), _stack_heads(k_t, low)], axis=0).astype(BF16)
    g = _dot_nt(lhs, rhs)
    n_mat = jnp.where(tri_strict, g[:L, :2 * L], 0.0)
    a_ak = jnp.where(tri_strict, g[:L, 2 * L:], 0.0)
    a_rb = jnp.where(tri_incl, g[L:, :2 * L], 0.0)
    a_rk = jnp.where(tri_incl, g[L:, 2 * L:], 0.0)

    n_pow = _dot(n_mat.astype(BF16), _stack_heads(n_mat, low).astype(BF16))
    t_mat = eye_cat + n_mat
    terms = 2
    while terms < L:
        sq = _stack_heads(n_pow, low).astype(BF16)
        if 2 * terms < L:
            both = _dot(jnp.concatenate([t_mat, n_pow], axis=0).astype(BF16), sq)
            t_mat = t_mat + both[:L]
            n_pow = both[L:]
        else:
            t_mat = t_mat + _dot(t_mat.astype(BF16), sq)
        terms *= 2

    vs = _stack_heads(v, low).astype(BF16)
    av_both = _dot(jnp.concatenate([a_ak, a_rk], axis=0).astype(BF16), vs)
    akv = av_both[:L]
    y0 = av_both[L:]
    rhs_t = jnp.concatenate([_stack_heads(a_t, low), _stack_heads(akv, low)], axis=1).astype(BF16)
    t_both = _dot(t_mat.astype(BF16), rhs_t)
    ta = t_both[:, :LANES]
    w2 = t_both[:, LANES:]

    s_both = _dot_nt(jnp.concatenate([ta, r_t], axis=0).astype(BF16), s_prev.astype(BF16))
    u = s_both[:L] + w2
    y = s_both[L:] + y0 + _dot(a_rb.astype(BF16), _stack_heads(u, low).astype(BF16))
    ds = _dot_tn(jnp.concatenate([u, v], axis=0).astype(BF16),
                 jnp.concatenate([b_t * p_last, k_t * p_last], axis=0).astype(BF16))
    s_new = s_prev * p_last + jnp.where(blockdiag, ds, 0.0)
    return y, s_new


def _split3(x):
    hi = x.astype(BF16)
    r1 = x - hi.astype(F32)
    mid = r1.astype(BF16)
    lo = (r1 - mid.astype(F32)).astype(BF16)
    return hi, mid, lo


def _rwkv_kernel(r_ref, k_ref, v_ref, wa_ref, mr_ref, mk_ref, mv_ref, mwa_ref,
                 mu_ref, w0_ref, a0_ref, lora_ref, kk_ref, ka_ref, rk_ref, lnw_ref, lnb_ref,
                 o_ref, s_ref, carry_ref, *, n_pairs):
    c = pl.program_id(1)
    L = CHUNK
    d = n_pairs * LANES

    @pl.when(c == 0)
    def _():
        s_ref[...] = jnp.zeros_like(s_ref)
        carry_ref[...] = jnp.zeros_like(carry_ref)

    is_meta = c == 0
    z = jnp.concatenate([
        jnp.where(is_meta, mr_ref[...], r_ref[...]),
        jnp.where(is_meta, mk_ref[...], k_ref[...]),
        jnp.where(is_meta, mv_ref[...], v_ref[...]),
        jnp.where(is_meta, mwa_ref[...], wa_ref[...]),
    ], axis=1).astype(F32)
    row = lax.broadcasted_iota(jnp.int32, z.shape, 0)
    z_prev = jnp.where(row == 0, carry_ref[...], pltpu.roll(z, 1, axis=0))
    carry_ref[...] = z[L - 1:L, :]
    z = z + (z_prev - z) * mu_ref[...]
    r_all = z[:, :d]
    k_all = z[:, d:2 * d]
    v_all = z[:, 2 * d:3 * d]
    wa = z[:, 3 * d:]

    lane = lax.broadcasted_iota(jnp.int32, (L, LANES), 1)
    low = lane < HEAD
    lora_in = jnp.where(low, jnp.tanh(wa), wa).astype(BF16)
    lora = _dot(lora_in, lora_ref[...])
    lw_all = -jnp.exp(F32(-0.5)) * jax.nn.sigmoid(w0_ref[...] + lora[:, :d])
    a_all = jax.nn.sigmoid(a0_ref[...] + lora[:, d:])

    row_c = lax.broadcasted_iota(jnp.int32, (L, LANES), 0)
    col_c = lane & (HEAD - 1)
    tri_strict = row_c > col_c
    tri_incl = row_c >= col_c
    eye_cat = (row_c == col_c).astype(F32)
    rr = lax.broadcasted_iota(jnp.int32, (LANES, LANES), 0)
    cc = lax.broadcasted_iota(jnp.int32, (LANES, LANES), 1)
    blockdiag = (rr < HEAD) == (cc < HEAD)
    ones_blk = blockdiag.astype(BF16)
    masks = (low, tri_strict, tri_incl, eye_cat, blockdiag)

    def seg_sum(x):
        tiles = jnp.concatenate([x[:, i * LANES:(i + 1) * LANES] for i in range(n_pairs)], axis=0)
        hi = tiles.astype(BF16)
        lo = (tiles - hi.astype(F32)).astype(BF16)
        s = _dot(hi, ones_blk) + _dot(lo, ones_blk)
        return jnp.concatenate([s[i * L:(i + 1) * L] for i in range(n_pairs)], axis=1)

    kk = k_all * kk_ref[...]
    kk = kk / jnp.maximum(jnp.sqrt(seg_sum(kk * kk)), 1e-12)
    k2_all = k_all * (1.0 + (a_all - 1.0) * ka_ref[...])
    av_all = -kk
    bv_all = kk * a_all

    tri_rows = (lax.broadcasted_iota(jnp.int32, (L, L), 0) >= lax.broadcasted_iota(jnp.int32, (L, L), 1)).astype(BF16)
    hi, mid, lo = _split3(lw_all)
    cum_all = _dot(tri_rows, hi) + _dot(tri_rows, mid) + _dot(tri_rows, lo)

    ys = []
    for i in range(n_pairs):
        sl = slice(i * LANES, (i + 1) * LANES)
        y, s_new = _rwkv_pair(r_all[:, sl], k2_all[:, sl], v_all[:, sl], lw_all[:, sl], cum_all[:, sl],
                              av_all[:, sl], bv_all[:, sl], s_ref[i], masks)
        s_ref[i] = s_new
        ys.append(y)
    y_all = jnp.concatenate(ys, axis=1)

    mean = seg_sum(y_all) * (1.0 / HEAD)
    dy = y_all - mean
    var = seg_sum(dy * dy) * (1.0 / HEAD)
    yn = dy * lax.rsqrt(var + GN_EPS) * lnw_ref[...] + lnb_ref[...]
    bonus = seg_sum(r_all * k2_all * rk_ref[...]) * v_all
    o_ref[...] = (yn + bonus).astype(o_ref.dtype)


def _rwkv(p_main, p_tail, pm_meta, pt_meta, params, batch, seq):
    d = p_main.shape[1] // 8
    n_pairs = d // LANES
    nc = seq // CHUNK
    meta_row = BLOCK // CHUNK - 1
    mu, w0, a0, lora_w, k_k, k_a, r_k, ln_w, ln_b = params
    kern = functools.partial(_rwkv_kernel, n_pairs=n_pairs)

    def main_spec(col):
        return pl.BlockSpec((CHUNK, d), lambda b, c: (b * nc + jnp.maximum(c - 1, 0), col))

    def meta_spec(col):
        return pl.BlockSpec((CHUNK, d), lambda b, c: (meta_row, col))

    def vec_spec(w):
        return pl.BlockSpec((1, w), lambda b, c: (0, 0))

    return pl.pallas_call(
        kern,
        grid=(batch, nc + 1),
        in_specs=[
            main_spec(2), main_spec(3), main_spec(4),
            pl.BlockSpec((CHUNK, LANES), lambda b, c: (b * nc + jnp.maximum(c - 1, 0), 2)),
            meta_spec(2), meta_spec(3), meta_spec(4),
            pl.BlockSpec((CHUNK, LANES), lambda b, c: (meta_row, 2)),
            vec_spec(3 * d + LANES), vec_spec(d), vec_spec(d),
            pl.BlockSpec((LANES, 2 * d), lambda b, c: (0, 0)),
            vec_spec(d), vec_spec(d), vec_spec(d), vec_spec(d), vec_spec(d),
        ],
        out_specs=pl.BlockSpec((CHUNK, d), lambda b, c: (b * nc + jnp.maximum(c - 1, 0), 0)),
        out_shape=jax.ShapeDtypeStruct((batch * seq, d), BF16),
        scratch_shapes=[pltpu.VMEM((n_pairs, LANES, LANES), F32), pltpu.VMEM((1, 3 * d + LANES), F32)],
        compiler_params=pltpu.CompilerParams(
            dimension_semantics=("parallel", "arbitrary"), vmem_limit_bytes=VMEM_LIMIT),
        name="rwkv7_chunked",
    )(p_main, p_main, p_main, p_tail, pm_meta, pm_meta, pm_meta, pt_meta,
      mu, w0, a0, lora_w, k_k, k_a, r_k, ln_w, ln_b)


def _merge_kernel(x_ref, att_ref, ga_ref, rw_ref, gr_ref, ma_ref, mr_ref, wa_ref, wr_ref, wo_ref, fw_ref, o_ref):
    def silu(t):
        return t * jax.nn.sigmoid(t)

    ga = ga_ref[...].astype(F32)
    gr = gr_ref[...].astype(F32)
    ya = _dot((att_ref[...].astype(F32) * silu(ga)).astype(BF16), wa_ref[...])
    yr = _dot((rw_ref[...].astype(F32) * silu(gr)).astype(BF16), wr_ref[...])
    merged = jax.nn.sigmoid(ma_ref[...].astype(F32)) * ya + jax.nn.sigmoid(mr_ref[...].astype(F32)) * yr
    h = x_ref[...] + _dot(merged.astype(BF16), wo_ref[...])
    ms = jnp.mean(h * h, axis=-1, keepdims=True)
    o_ref[...] = h * lax.rsqrt(ms + RMS_EPS) * fw_ref[...]


def _merge(x2d, att, p_main, rw, w_a, w_r, w_o, final_w):
    m, d = x2d.shape
    tm = min(_row_tile(m), 512)

    def rows(col):
        return pl.BlockSpec((tm, d), lambda i: (i, col))

    def full():
        return pl.BlockSpec((d, d), lambda i: (0, 0))

    return pl.pallas_call(
        _merge_kernel,
        grid=(m // tm,),
        in_specs=[rows(0), rows(0), rows(1), rows(0), rows(5), rows(6), rows(7), full(), full(), full(),
                  pl.BlockSpec((1, d), lambda i: (0, 0))],
        out_specs=rows(0),
        out_shape=jax.ShapeDtypeStruct((m, d), F32),
        compiler_params=pltpu.CompilerParams(
            dimension_semantics=("parallel",), vmem_limit_bytes=VMEM_LIMIT),
        name="merge_out",
    )(x2d, att, p_main, rw, p_main, p_main, p_main, w_a, w_r, w_o, final_w)


def _rope_table(pos):
    half = HEAD // 2
    inv = 1.0 / (ROPE_THETA ** (jnp.arange(half, dtype=F32) / half))
    ang = pos[:, None] * inv[None, :]
    cos = jnp.cos(ang)
    sin = jnp.sin(ang)
    cos_pair = jnp.concatenate([cos, cos, cos, cos], axis=1)
    sin_pair = jnp.concatenate([-sin, sin, -sin, sin], axis=1)
    return jnp.concatenate([cos_pair, sin_pair], axis=1)


def kernel(x, meta_tokens, norm_w, w_in, att_sinks, rk_mu, rk_w0, rk_w2, rk_a0, rk_a2, rk_k_k, rk_k_a, rk_r_k,
           rk_ln_w, rk_ln_b, w_branch_att, w_branch_rwkv, w_out, final_norm_w):
    batch, seq, d = x.shape
    assert seq % BLOCK == 0 and d % LANES == 0 and norm_w.shape[0] == 1
    kvw = KV_HEADS * HEAD
    assert kvw == LANES and 2 * LORA == LANES

    w = w_in[0]
    o = 0
    cols = {}
    for name, size in (("q", d), ("ka", kvw), ("va", kvw), ("ga", d), ("r", d), ("kr", d), ("vr", d),
                       ("wa", 2 * LORA), ("gr", d), ("ma", d), ("mr", d)):
        cols[name] = w[:, o:o + size]
        o += size
    w_main = jnp.concatenate([cols[n] for n in ("q", "ga", "r", "kr", "vr", "gr", "ma", "mr")], axis=1).astype(BF16)
    w_tail = jnp.concatenate([cols[n] for n in ("ka", "va", "wa")], axis=1).astype(BF16)

    nw = norm_w[0][None, :]
    x2d = x.reshape(batch * seq, d)
    meta_blk = jnp.concatenate([jnp.zeros((BLOCK - N_META, d), x.dtype), meta_tokens.astype(x.dtype)], axis=0)

    cs_x = _rope_table(N_META + jnp.arange(seq, dtype=F32))
    cs_meta = _rope_table(jnp.maximum(jnp.arange(BLOCK, dtype=F32) - (BLOCK - N_META), 0.0))
    p_main, p_tail = _inproj(x2d, nw, cs_x, w_main, w_tail)
    pm_meta, pt_meta = _inproj(meta_blk, nw, cs_meta, w_main, w_tail)

    att = _attention(att_sinks[0], p_main, p_tail, pt_meta, batch, seq)

    zeros_lora = jnp.zeros((LORA, d), F32)
    lora_w = jnp.concatenate([
        jnp.concatenate([rk_w2[0], zeros_lora], axis=0),
        jnp.concatenate([zeros_lora, rk_a2[0]], axis=0)], axis=1).astype(BF16)
    params = (rk_mu[0][None, :], rk_w0[0][None, :], rk_a0[0][None, :], lora_w, rk_k_k[0][None, :],
              rk_k_a[0][None, :], rk_r_k[0].reshape(1, d), rk_ln_w[0][None, :], rk_ln_b[0][None, :])
    rw = _rwkv(p_main, p_tail, pm_meta, pt_meta, params, batch, seq)

    y = _merge(x2d, att, p_main, rw, w_branch_att[0].astype(BF16), w_branch_rwkv[0].astype(BF16),
               w_out[0].astype(BF16), final_norm_w[None, :])
    return y.reshape(batch, seq, d)
```

```python
import functools

import jax
import jax.numpy as jnp
from jax import lax
from jax.experimental import pallas as pl
from jax.experimental.pallas import tpu as pltpu

F32 = jnp.float32
BF16 = jnp.bfloat16

N_META = 16
BLOCK = 128
HEAD = 64
KV_HEADS = 2
LORA = 64
CHUNK = 64
ROPE_THETA = 10000.0
RMS_EPS = 1e-6
GN_EPS = 64e-5
NEG_INF = -1e30
LANES = 128
VMEM_LIMIT = 48 * 1024 * 1024

NT_DIMS = (((1,), (1,)), ((), ()))
TN_DIMS = (((0,), (0,)), ((), ()))


def _dot(a, b):
    return jnp.dot(a, b, preferred_element_type=F32)


def _dot_nt(a, b):
    return lax.dot_general(a, b, NT_DIMS, preferred_element_type=F32)


def _dot_tn(a, b):
    return lax.dot_general(a, b, TN_DIMS, preferred_element_type=F32)


def _row_tile(m):
    for t in (1024, 512, 256, 128):
        if m % t == 0:
            return t
    raise ValueError(f"row count {m} must be a multiple of 128")


def _rope(acc, cs, reps):
    width = acc.shape[1]
    cos = cs[:, :LANES]
    sin = cs[:, LANES:]
    if reps > 1:
        cos = jnp.tile(cos, (1, reps))
        sin = jnp.tile(sin, (1, reps))
    lane = lax.broadcasted_iota(jnp.int32, acc.shape, 1)
    first_half = (lane & (HEAD - 1)) < (HEAD // 2)
    rot = jnp.where(first_half, pltpu.roll(acc, width - HEAD // 2, axis=1), pltpu.roll(acc, HEAD // 2, axis=1))
    return acc * cos + rot * sin


def _inproj_kernel(x_ref, nw_ref, cs_ref, wm_ref, wt_ref, pm_ref, pt_ref, xn_ref, *, n_main):
    j = pl.program_id(1)

    @pl.when(j == 0)
    def _():
        x = x_ref[...]
        ms = jnp.mean(x * x, axis=-1, keepdims=True)
        xn_ref[...] = (x * lax.rsqrt(ms + RMS_EPS) * nw_ref[...]).astype(BF16)
        acc = _dot(xn_ref[...], wm_ref[...])
        pm_ref[...] = _rope(acc, cs_ref[...], acc.shape[1] // LANES).astype(BF16)

    @pl.when(jnp.logical_and(j > 0, j < n_main))
    def _():
        pm_ref[...] = _dot(xn_ref[...], wm_ref[...]).astype(BF16)

    @pl.when(j == n_main)
    def _():
        acc = _dot(xn_ref[...], wt_ref[...])
        ka = _rope(acc[:, :LANES], cs_ref[...], 1)
        pt_ref[:, :LANES] = ka.astype(BF16)
        pt_ref[:, LANES:] = acc[:, LANES:].astype(BF16)


def _inproj(x2d, norm_w, cs, w_main, w_tail):
    m, d = x2d.shape
    tm = _row_tile(cs.shape[0])
    cs_period_tiles = cs.shape[0] // tm
    n_main = w_main.shape[1] // d
    tail = w_tail.shape[1]
    kern = functools.partial(_inproj_kernel, n_main=n_main)
    return pl.pallas_call(
        kern,
        grid=(m // tm, n_main + 1),
        in_specs=[
            pl.BlockSpec((tm, d), lambda i, j: (i, 0)),
            pl.BlockSpec((1, d), lambda i, j: (0, 0)),
            pl.BlockSpec((tm, 2 * LANES), lambda i, j: (i % cs_period_tiles, 0)),
            pl.BlockSpec((d, d), lambda i, j: (0, jnp.minimum(j, n_main - 1))),
            pl.BlockSpec((d, tail), lambda i, j: (0, 0)),
        ],
        out_specs=[
            pl.BlockSpec((tm, d), lambda i, j: (i, jnp.minimum(j, n_main - 1))),
            pl.BlockSpec((tm, tail), lambda i, j: (i, 0)),
        ],
        out_shape=[
            jax.ShapeDtypeStruct((m, n_main * d), BF16),
            jax.ShapeDtypeStruct((m, tail), BF16),
        ],
        scratch_shapes=[pltpu.VMEM((tm, d), BF16)],
        compiler_params=pltpu.CompilerParams(
            dimension_semantics=("parallel", "arbitrary"), vmem_limit_bytes=VMEM_LIMIT),
        name="inproj",
    )(x2d, norm_w, cs, w_main, w_tail)


def _attn_kernel(sink_ref, q_ref, kvp_ref, kvc_ref, kvm_ref, o_ref, *, n_pairs, group_pairs, scale):
    n = pl.program_id(1)
    prev = jnp.where(n == 0, kvm_ref[...], kvp_ref[...])
    kv = jnp.concatenate([prev, kvc_ref[...]], axis=0).astype(F32)
    k = kv[:, :LANES]
    v = kv[:, LANES:2 * LANES]
    lane_kv = lax.broadcasted_iota(jnp.int32, k.shape, 1)
    low = lane_kv < HEAD
    k_sw = pltpu.roll(k, HEAD, axis=1)
    v_sw = pltpu.roll(v, HEAD, axis=1)
    k_dup = [jnp.where(low, k, k_sw).astype(BF16), jnp.where(low, k_sw, k).astype(BF16)]
    v_dup = [jnp.where(low, v, v_sw).astype(BF16), jnp.where(low, v_sw, v).astype(BF16)]

    rows = 2 * BLOCK
    r = lax.broadcasted_iota(jnp.int32, (rows, rows), 0) & (BLOCK - 1)
    c = lax.broadcasted_iota(jnp.int32, (rows, rows), 1)
    first_real = jnp.where(n == 0, BLOCK - N_META, 0)
    ok = (c > r) & (c <= r + BLOCK) & (c >= first_real)
    top = lax.broadcasted_iota(jnp.int32, (rows, 1), 0) < BLOCK
    lane_q = lax.broadcasted_iota(jnp.int32, (BLOCK, LANES), 1)
    low_q = lane_q < HEAD

    for i in range(n_pairs):
        g = i // group_pairs
        q2 = q_ref[:, i * LANES:(i + 1) * LANES]
        zero = jnp.zeros_like(q2)
        qs = jnp.concatenate([jnp.where(low_q, q2, zero), jnp.where(low_q, zero, q2)], axis=0)
        s = _dot_nt(qs, k_dup[g]) * scale
        s = jnp.where(ok, s, NEG_INF)
        sink = jnp.where(top, sink_ref[2 * i], sink_ref[2 * i + 1])
        mx = jnp.maximum(jnp.max(s, axis=-1, keepdims=True), sink)
        e = jnp.exp(s - mx)
        den = jnp.sum(e, axis=-1, keepdims=True) + jnp.exp(sink - mx)
        o = _dot(e.astype(BF16), v_dup[g]) * (1.0 / den)
        o_ref[:, i * LANES:(i + 1) * LANES] = jnp.where(low_q, o[:BLOCK], o[BLOCK:]).astype(o_ref.dtype)


def _attention(sinks, p_main, p_tail, pt_meta, batch, seq):
    d = p_main.shape[1] // 8
    tail = p_tail.shape[1]
    nbq = seq // BLOCK
    n_pairs = d // LANES
    kern = functools.partial(_attn_kernel, n_pairs=n_pairs, group_pairs=n_pairs // KV_HEADS, scale=HEAD ** -0.5)
    return pl.pallas_call(
        kern,
        grid=(batch, nbq),
        in_specs=[
            pl.BlockSpec(memory_space=pltpu.SMEM),
            pl.BlockSpec((BLOCK, d), lambda b, n: (b * nbq + n, 0)),
            pl.BlockSpec((BLOCK, tail), lambda b, n: (b * nbq + jnp.maximum(n - 1, 0), 0)),
            pl.BlockSpec((BLOCK, tail), lambda b, n: (b * nbq + n, 0)),
            pl.BlockSpec((BLOCK, tail), lambda b, n: (0, 0)),
        ],
        out_specs=pl.BlockSpec((BLOCK, d), lambda b, n: (b * nbq + n, 0)),
        out_shape=jax.ShapeDtypeStruct((batch * seq, d), BF16),
        compiler_params=pltpu.CompilerParams(
            dimension_semantics=("parallel", "arbitrary"), vmem_limit_bytes=VMEM_LIMIT),
        name="swa_attention",
    )(sinks, p_main, p_tail, p_tail, pt_meta)


def _stack_heads(x, low):
    zero = jnp.zeros_like(x)
    return jnp.concatenate([jnp.where(low, x, zero), jnp.where(low, zero, x)], axis=0)


def _rwkv_pairs(r, k2, v, lw, cum, av, bv, s_prev, masks):
    low, tri_strict, tri_incl, eye_cat, blockdiag = masks
    n = len(r)
    L = r[0].shape[0]
    idx = range(n)
    p_incl = [jnp.exp(cum[i]) for i in idx]
    p_excl = [jnp.exp(cum[i] - lw[i]) for i in idx]
    p_inv = [jnp.exp(-cum[i]) for i in idx]
    p_last = [p_incl[i][L - 1:L, :] for i in idx]
    a_t = [av[i] * p_excl[i] for i in idx]
    r_t = [r[i] * p_incl[i] for i in idx]
    b_t = [bv[i] * p_inv[i] for i in idx]
    k_t = [k2[i] * p_inv[i] for i in idx]

    g = [_dot_nt(jnp.concatenate([a_t[i], r_t[i]], axis=0).astype(BF16),
                 jnp.concatenate([_stack_heads(b_t[i], low), _stack_heads(k_t[i], low)], axis=0).astype(BF16))
         for i in idx]
    n_mat = [jnp.where(tri_strict, g[i][:L, :2 * L], 0.0) for i in idx]
    a_ak = [jnp.where(tri_strict, g[i][:L, 2 * L:], 0.0) for i in idx]
    a_rb = [jnp.where(tri_incl, g[i][L:, :2 * L], 0.0) for i in idx]
    a_rk = [jnp.where(tri_incl, g[i][L:, 2 * L:], 0.0) for i in idx]

    vs = [_stack_heads(v[i], low).astype(BF16) for i in idx]
    av_both = [_dot(jnp.concatenate([a_ak[i], a_rk[i]], axis=0).astype(BF16), vs[i]) for i in idx]
    akv = [av_both[i][:L] for i in idx]
    y0 = [av_both[i][L:] for i in idx]

    n_pow = [_dot(n_mat[i].astype(BF16), _stack_heads(n_mat[i], low).astype(BF16)) for i in idx]
    t_mat = [eye_cat + n_mat[i] for i in idx]
    terms = 2
    while terms < L:
        sq = [_stack_heads(n_pow[i], low).astype(BF16) for i in idx]
        if 2 * terms < L:
            both = [_dot(jnp.concatenate([t_mat[i], n_pow[i]], axis=0).astype(BF16), sq[i]) for i in idx]
            t_mat = [t_mat[i] + both[i][:L] for i in idx]
            n_pow = [both[i][L:] for i in idx]
        else:
            t_mat = [t_mat[i] + _dot(t_mat[i].astype(BF16), sq[i]) for i in idx]
        terms *= 2

    t_both = [_dot(t_mat[i].astype(BF16),
                   jnp.concatenate([_stack_heads(a_t[i], low), _stack_heads(akv[i], low)], axis=1).astype(BF16))
              for i in idx]
    s_both = [_dot_nt(jnp.concatenate([t_both[i][:, :LANES], r_t[i]], axis=0).astype(BF16), s_prev[i].astype(BF16))
              for i in idx]
    u = [s_both[i][:L] + t_both[i][:, LANES:] for i in idx]
    y = [s_both[i][L:] + y0[i] + _dot(a_rb[i].astype(BF16), _stack_heads(u[i], low).astype(BF16)) for i in idx]
    ds = [_dot_tn(jnp.concatenate([u[i], v[i]], axis=0).astype(BF16),
                  jnp.concatenate([b_t[i] * p_last[i], k_t[i] * p_last[i]], axis=0).astype(BF16)) for i in idx]
    s_new = [s_prev[i] * p_last[i] + jnp.where(blockdiag, ds[i], 0.0) for i in idx]
    return y, s_new


def _split3(x):
    hi = x.astype(BF16)
    r1 = x - hi.astype(F32)
    mid = r1.astype(BF16)
    lo = (r1 - mid.astype(F32)).astype(BF16)
    return hi, mid, lo


def _rwkv_kernel(r_ref, k_ref, v_ref, wa_ref, mr_ref, mk_ref, mv_ref, mwa_ref,
                 mu_ref, w0_ref, a0_ref, lora_ref, kk_ref, ka_ref, rk_ref, lnw_ref, lnb_ref,
                 o_ref, s_ref, carry_ref, *, n_pairs):
    c = pl.program_id(1)
    L = CHUNK
    d = n_pairs * LANES

    @pl.when(c == 0)
    def _():
        s_ref[...] = jnp.zeros_like(s_ref)
        carry_ref[...] = jnp.zeros_like(carry_ref)

    is_meta = c == 0
    z = jnp.concatenate([
        jnp.where(is_meta, mr_ref[...], r_ref[...]),
        jnp.where(is_meta, mk_ref[...], k_ref[...]),
        jnp.where(is_meta, mv_ref[...], v_ref[...]),
        jnp.where(is_meta, mwa_ref[...], wa_ref[...]),
    ], axis=1).astype(F32)
    row = lax.broadcasted_iota(jnp.int32, z.shape, 0)
    z_prev = jnp.where(row == 0, carry_ref[...], pltpu.roll(z, 1, axis=0))
    carry_ref[...] = z[L - 1:L, :]
    z = z + (z_prev - z) * mu_ref[...]
    r_all = z[:, :d]
    k_all = z[:, d:2 * d]
    v_all = z[:, 2 * d:3 * d]
    wa = z[:, 3 * d:]

    lane = lax.broadcasted_iota(jnp.int32, (L, LANES), 1)
    low = lane < HEAD
    lora_in = jnp.where(low, jnp.tanh(wa), wa).astype(BF16)
    lora = _dot(lora_in, lora_ref[...])
    lw_all = -jnp.exp(F32(-0.5)) * jax.nn.sigmoid(w0_ref[...] + lora[:, :d])
    a_all = jax.nn.sigmoid(a0_ref[...] + lora[:, d:])

    row_c = lax.broadcasted_iota(jnp.int32, (L, LANES), 0)
    col_c = lane & (HEAD - 1)
    tri_strict = row_c > col_c
    tri_incl = row_c >= col_c
    eye_cat = (row_c == col_c).astype(F32)
    rr = lax.broadcasted_iota(jnp.int32, (LANES, LANES), 0)
    cc = lax.broadcasted_iota(jnp.int32, (LANES, LANES), 1)
    blockdiag = (rr < HEAD) == (cc < HEAD)
    ones_blk = blockdiag.astype(BF16)
    masks = (low, tri_strict, tri_incl, eye_cat, blockdiag)

    def seg_sum(x):
        tiles = jnp.concatenate([x[:, i * LANES:(i + 1) * LANES] for i in range(n_pairs)], axis=0)
        hi = tiles.astype(BF16)
        lo = (tiles - hi.astype(F32)).astype(BF16)
        s = _dot(hi, ones_blk) + _dot(lo, ones_blk)
        return jnp.concatenate([s[i * L:(i + 1) * L] for i in range(n_pairs)], axis=1)

    kk = k_all * kk_ref[...]
    kk = kk / jnp.maximum(jnp.sqrt(seg_sum(kk * kk)), 1e-12)
    k2_all = k_all * (1.0 + (a_all - 1.0) * ka_ref[...])
    av_all = -kk
    bv_all = kk * a_all

    tri_rows = (lax.broadcasted_iota(jnp.int32, (L, L), 0) >= lax.broadcasted_iota(jnp.int32, (L, L), 1)).astype(BF16)
    hi, mid, lo = _split3(lw_all)
    cum_all = _dot(tri_rows, hi) + _dot(tri_rows, mid) + _dot(tri_rows, lo)

    def tiles(x):
        return [x[:, i * LANES:(i + 1) * LANES] for i in range(n_pairs)]

    ys, s_new = _rwkv_pairs(tiles(r_all), tiles(k2_all), tiles(v_all), tiles(lw_all), tiles(cum_all),
                            tiles(av_all), tiles(bv_all), [s_ref[i] for i in range(n_pairs)], masks)
    for i in range(n_pairs):
        s_ref[i] = s_new[i]
    y_all = jnp.concatenate(ys, axis=1)

    mean = seg_sum(y_all) * (1.0 / HEAD)
    dy = y_all - mean
    var = seg_sum(dy * dy) * (1.0 / HEAD)
    yn = dy * lax.rsqrt(var + GN_EPS) * lnw_ref[...] + lnb_ref[...]
    bonus = seg_sum(r_all * k2_all * rk_ref[...]) * v_all
    o_ref[...] = (yn + bonus).astype(o_ref.dtype)


def _rwkv(p_main, p_tail, pm_meta, pt_meta, params, batch, seq):
    d = p_main.shape[1] // 8
    n_pairs = d // LANES
    nc = seq // CHUNK
    meta_row = BLOCK // CHUNK - 1
    mu, w0, a0, lora_w, k_k, k_a, r_k, ln_w, ln_b = params
    kern = functools.partial(_rwkv_kernel, n_pairs=n_pairs)

    def main_spec(col):
        return pl.BlockSpec((CHUNK, d), lambda b, c: (b * nc + jnp.maximum(c - 1, 0), col))

    def meta_spec(col):
        return pl.BlockSpec((CHUNK, d), lambda b, c: (meta_row, col))

    def vec_spec(w):
        return pl.BlockSpec((1, w), lambda b, c: (0, 0))

    return pl.pallas_call(
        kern,
        grid=(batch, nc + 1),
        in_specs=[
            main_spec(2), main_spec(3), main_spec(4),
            pl.BlockSpec((CHUNK, LANES), lambda b, c: (b * nc + jnp.maximum(c - 1, 0), 2)),
            meta_spec(2), meta_spec(3), meta_spec(4),
            pl.BlockSpec((CHUNK, LANES), lambda b, c: (meta_row, 2)),
            vec_spec(3 * d + LANES), vec_spec(d), vec_spec(d),
            pl.BlockSpec((LANES, 2 * d), lambda b, c: (0, 0)),
            vec_spec(d), vec_spec(d), vec_spec(d), vec_spec(d), vec_spec(d),
        ],
        out_specs=pl.BlockSpec((CHUNK, d), lambda b, c: (b * nc + jnp.maximum(c - 1, 0), 0)),
        out_shape=jax.ShapeDtypeStruct((batch * seq, d), BF16),
        scratch_shapes=[pltpu.VMEM((n_pairs, LANES, LANES), F32), pltpu.VMEM((1, 3 * d + LANES), F32)],
        compiler_params=pltpu.CompilerParams(
            dimension_semantics=("parallel", "arbitrary"), vmem_limit_bytes=VMEM_LIMIT),
        name="rwkv7_chunked",
    )(p_main, p_main, p_main, p_tail, pm_meta, pm_meta, pm_meta, pt_meta,
      mu, w0, a0, lora_w, k_k, k_a, r_k, ln_w, ln_b)


def _merge_kernel(x_ref, att_ref, ga_ref, rw_ref, gr_ref, ma_ref, mr_ref, wa_ref, wr_ref, wo_ref, fw_ref, o_ref):
    def silu(t):
        return t * jax.nn.sigmoid(t)

    ga = ga_ref[...].astype(F32)
    gr = gr_ref[...].astype(F32)
    ya = _dot((att_ref[...].astype(F32) * silu(ga)).astype(BF16), wa_ref[...])
    yr = _dot((rw_ref[...].astype(F32) * silu(gr)).astype(BF16), wr_ref[...])
    merged = jax.nn.sigmoid(ma_ref[...].astype(F32)) * ya + jax.nn.sigmoid(mr_ref[...].astype(F32)) * yr
    h = x_ref[...] + _dot(merged.astype(BF16), wo_ref[...])
    ms = jnp.mean(h * h, axis=-1, keepdims=True)
    o_ref[...] = h * lax.rsqrt(ms + RMS_EPS) * fw_ref[...]


def _merge(x2d, att, p_main, rw, w_a, w_r, w_o, final_w):
    m, d = x2d.shape
    tm = min(_row_tile(m), 512)

    def rows(col):
        return pl.BlockSpec((tm, d), lambda i: (i, col))

    def full():
        return pl.BlockSpec((d, d), lambda i: (0, 0))

    return pl.pallas_call(
        _merge_kernel,
        grid=(m // tm,),
        in_specs=[rows(0), rows(0), rows(1), rows(0), rows(5), rows(6), rows(7), full(), full(), full(),
                  pl.BlockSpec((1, d), lambda i: (0, 0))],
        out_specs=rows(0),
        out_shape=jax.ShapeDtypeStruct((m, d), F32),
        compiler_params=pltpu.CompilerParams(
            dimension_semantics=("parallel",), vmem_limit_bytes=VMEM_LIMIT),
        name="merge_out",
    )(x2d, att, p_main, rw, p_main, p_main, p_main, w_a, w_r, w_o, final_w)


def _rope_table(pos):
    half = HEAD // 2
    inv = 1.0 / (ROPE_THETA ** (jnp.arange(half, dtype=F32) / half))
    ang = pos[:, None] * inv[None, :]
    cos = jnp.cos(ang)
    sin = jnp.sin(ang)
    cos_pair = jnp.concatenate([cos, cos, cos, cos], axis=1)
    sin_pair = jnp.concatenate([-sin, sin, -sin, sin], axis=1)
    return jnp.concatenate([cos_pair, sin_pair], axis=1)


def kernel(x, meta_tokens, norm_w, w_in, att_sinks, rk_mu, rk_w0, rk_w2, rk_a0, rk_a2, rk_k_k, rk_k_a, rk_r_k,
           rk_ln_w, rk_ln_b, w_branch_att, w_branch_rwkv, w_out, final_norm_w):
    batch, seq, d = x.shape
    assert seq % BLOCK == 0 and d % LANES == 0 and norm_w.shape[0] == 1
    kvw = KV_HEADS * HEAD
    assert kvw == LANES and 2 * LORA == LANES

    w = w_in[0]
    o = 0
    cols = {}
    for name, size in (("q", d), ("ka", kvw), ("va", kvw), ("ga", d), ("r", d), ("kr", d), ("vr", d),
                       ("wa", 2 * LORA), ("gr", d), ("ma", d), ("mr", d)):
        cols[name] = w[:, o:o + size]
        o += size
    w_main = jnp.concatenate([cols[n] for n in ("q", "ga", "r", "kr", "vr", "gr", "ma", "mr")], axis=1).astype(BF16)
    w_tail = jnp.concatenate([cols[n] for n in ("ka", "va", "wa")], axis=1).astype(BF16)

    nw = norm_w[0][None, :]
    x2d = x.reshape(batch * seq, d)
    meta_blk = jnp.concatenate([jnp.zeros((BLOCK - N_META, d), x.dtype), meta_tokens.astype(x.dtype)], axis=0)

    cs_x = _rope_table(N_META + jnp.arange(seq, dtype=F32))
    cs_meta = _rope_table(jnp.maximum(jnp.arange(BLOCK, dtype=F32) - (BLOCK - N_META), 0.0))
    p_main, p_tail = _inproj(x2d, nw, cs_x, w_main, w_tail)
    pm_meta, pt_meta = _inproj(meta_blk, nw, cs_meta, w_main, w_tail)

    att = _attention(att_sinks[0], p_main, p_tail, pt_meta, batch, seq)

    zeros_lora = jnp.zeros((LORA, d), F32)
    lora_w = jnp.concatenate([
        jnp.concatenate([rk_w2[0], zeros_lora], axis=0),
        jnp.concatenate([zeros_lora, rk_a2[0]], axis=0)], axis=1).astype(BF16)
    params = (rk_mu[0][None, :], rk_w0[0][None, :], rk_a0[0][None, :], lora_w, rk_k_k[0][None, :],
              rk_k_a[0][None, :], rk_r_k[0].reshape(1, d), rk_ln_w[0][None, :], rk_ln_b[0][None, :])
    rw = _rwkv(p_main, p_tail, pm_meta, pt_meta, params, batch, seq)

    y = _merge(x2d, att, p_main, rw, w_branch_att[0].astype(BF16), w_branch_rwkv[0].astype(BF16),
               w_out[0].astype(BF16), final_norm_w[None, :])
    return y.reshape(batch, seq, d)
```

```python
import functools
import math

import jax
import jax.numpy as jnp
from jax import lax
from jax.experimental import pallas as pl
from jax.experimental.pallas import tpu as pltpu

F32 = jnp.float32
BF16 = jnp.bfloat16

N_META = 16
BLOCK = 128
HEAD = 64
KV_HEADS = 2
LORA = 64
CHUNK = 64
ROPE_THETA = 10000.0
RMS_EPS = 1e-6
GN_EPS = 64e-5
NEG_INF = -1e30
LOG2_E = 1.4426950408889634
LANES = 128
VMEM_LIMIT = 48 * 1024 * 1024
RWKV_ROWS = 4
RWKV_LEAD = 4

NT_DIMS = (((1,), (1,)), ((), ()))
TN_DIMS = (((0,), (0,)), ((), ()))


def _sigmoid(t):
    return 0.5 * jnp.tanh(0.5 * t) + 0.5


def _dot(a, b):
    return jnp.dot(a, b, preferred_element_type=F32)


def _dot_nt(a, b):
    return lax.dot_general(a, b, NT_DIMS, preferred_element_type=F32)


def _dot_tn(a, b):
    return lax.dot_general(a, b, TN_DIMS, preferred_element_type=F32)


def _row_tile(m):
    for t in (1024, 512, 256, 128):
        if m % t == 0:
            return t
    raise ValueError(f"row count {m} must be a multiple of 128")


def _rope(acc, cs, reps):
    width = acc.shape[1]
    cos = cs[:, :LANES]
    sin = cs[:, LANES:]
    if reps > 1:
        cos = jnp.tile(cos, (1, reps))
        sin = jnp.tile(sin, (1, reps))
    lane = lax.broadcasted_iota(jnp.int32, acc.shape, 1)
    first_half = (lane & (HEAD - 1)) < (HEAD // 2)
    rot = jnp.where(first_half, pltpu.roll(acc, width - HEAD // 2, axis=1), pltpu.roll(acc, HEAD // 2, axis=1))
    return acc * cos + rot * sin


def _inproj_kernel(x_ref, nw_ref, cs_ref, wm_ref, wt_ref, pm_ref, pt_ref, xn_ref, *, n_main):
    j = pl.program_id(1)

    @pl.when(j == 0)
    def _():
        x = x_ref[...]
        ms = jnp.mean(x * x, axis=-1, keepdims=True)
        xn_ref[...] = (x * lax.rsqrt(ms + RMS_EPS) * nw_ref[...]).astype(BF16)
        acc = _dot(xn_ref[...], wm_ref[...])
        pm_ref[...] = _rope(acc, cs_ref[...], acc.shape[1] // LANES).astype(BF16)

    @pl.when(jnp.logical_and(j > 0, j < n_main))
    def _():
        pm_ref[...] = _dot(xn_ref[...], wm_ref[...]).astype(BF16)

    @pl.when(j == n_main)
    def _():
        acc = _dot(xn_ref[...], wt_ref[...])
        ka = _rope(acc[:, :LANES], cs_ref[...], 1)
        pt_ref[:, :LANES] = ka.astype(BF16)
        pt_ref[:, LANES:] = acc[:, LANES:].astype(BF16)


def _inproj(x2d, norm_w, cs, w_main, w_tail):
    m, d = x2d.shape
    tm = _row_tile(cs.shape[0])
    cs_period_tiles = cs.shape[0] // tm
    n_main = w_main.shape[1] // d
    tail = w_tail.shape[1]
    kern = functools.partial(_inproj_kernel, n_main=n_main)
    return pl.pallas_call(
        kern,
        grid=(m // tm, n_main + 1),
        in_specs=[
            pl.BlockSpec((tm, d), lambda i, j: (i, 0)),
            pl.BlockSpec((1, d), lambda i, j: (0, 0)),
            pl.BlockSpec((tm, 2 * LANES), lambda i, j: (i % cs_period_tiles, 0)),
            pl.BlockSpec((d, d), lambda i, j: (0, jnp.minimum(j, n_main - 1))),
            pl.BlockSpec((d, tail), lambda i, j: (0, 0)),
        ],
        out_specs=[
            pl.BlockSpec((tm, d), lambda i, j: (i, jnp.minimum(j, n_main - 1))),
            pl.BlockSpec((tm, tail), lambda i, j: (i, 0)),
        ],
        out_shape=[
            jax.ShapeDtypeStruct((m, n_main * d), BF16),
            jax.ShapeDtypeStruct((m, tail), BF16),
        ],
        scratch_shapes=[pltpu.VMEM((tm, d), BF16)],
        compiler_params=pltpu.CompilerParams(
            dimension_semantics=("parallel", "arbitrary"), vmem_limit_bytes=VMEM_LIMIT),
        name="inproj",
    )(x2d, norm_w, cs, w_main, w_tail)


def _attn_kernel(sink_ref, q_ref, kvp_ref, kvc_ref, kvm_ref, o_ref, *, n_pairs, group_pairs, scale):
    n = pl.program_id(1)
    prev = jnp.where(n == 0, kvm_ref[...], kvp_ref[...])
    kv = jnp.concatenate([prev, kvc_ref[...]], axis=0).astype(F32)
    k = kv[:, :LANES]
    v = kv[:, LANES:2 * LANES]
    lane_kv = lax.broadcasted_iota(jnp.int32, k.shape, 1)
    low = lane_kv < HEAD
    k_sw = pltpu.roll(k, HEAD, axis=1)
    v_sw = pltpu.roll(v, HEAD, axis=1)
    k_dup = [jnp.where(low, k, k_sw).astype(BF16), jnp.where(low, k_sw, k).astype(BF16)]
    v_dup = [jnp.where(low, v, v_sw).astype(BF16), jnp.where(low, v_sw, v).astype(BF16)]

    rows = 2 * BLOCK
    r = lax.broadcasted_iota(jnp.int32, (rows, rows), 0) & (BLOCK - 1)
    c = lax.broadcasted_iota(jnp.int32, (rows, rows), 1)
    first_real = jnp.where(n == 0, BLOCK - N_META, 0)
    ok = (c > r) & (c <= r + BLOCK) & (c >= first_real)
    top = lax.broadcasted_iota(jnp.int32, (rows, 1), 0) < BLOCK
    lane_q = lax.broadcasted_iota(jnp.int32, (BLOCK, LANES), 1)
    low_q = lane_q < HEAD

    for i in range(n_pairs):
        g = i // group_pairs
        q2 = q_ref[:, i * LANES:(i + 1) * LANES]
        zero = jnp.zeros_like(q2)
        qs = jnp.concatenate([jnp.where(low_q, q2, zero), jnp.where(low_q, zero, q2)], axis=0)
        s = _dot_nt(qs, k_dup[g]) * scale
        s = jnp.where(ok, s, NEG_INF)
        sink = jnp.where(top, sink_ref[2 * i], sink_ref[2 * i + 1])
        mx = jnp.maximum(jnp.max(s, axis=-1, keepdims=True), sink)
        e = jnp.exp(s - mx)
        den = jnp.sum(e, axis=-1, keepdims=True) + jnp.exp(sink - mx)
        o = _dot(e.astype(BF16), v_dup[g]) * (1.0 / den)
        o_ref[:, i * LANES:(i + 1) * LANES] = jnp.where(low_q, o[:BLOCK], o[BLOCK:]).astype(o_ref.dtype)


def _attention(sinks, p_main, p_tail, pt_meta, batch, seq):
    d = p_main.shape[1] // 8
    tail = p_tail.shape[1]
    nbq = seq // BLOCK
    n_pairs = d // LANES
    kern = functools.partial(_attn_kernel, n_pairs=n_pairs, group_pairs=n_pairs // KV_HEADS, scale=HEAD ** -0.5)
    return pl.pallas_call(
        kern,
        grid=(batch, nbq),
        in_specs=[
            pl.BlockSpec(memory_space=pltpu.SMEM),
            pl.BlockSpec((BLOCK, d), lambda b, n: (b * nbq + n, 0)),
            pl.BlockSpec((BLOCK, tail), lambda b, n: (b * nbq + jnp.maximum(n - 1, 0), 0)),
            pl.BlockSpec((BLOCK, tail), lambda b, n: (b * nbq + n, 0)),
            pl.BlockSpec((BLOCK, tail), lambda b, n: (0, 0)),
        ],
        out_specs=pl.BlockSpec((BLOCK, d), lambda b, n: (b * nbq + n, 0)),
        out_shape=jax.ShapeDtypeStruct((batch * seq, d), BF16),
        compiler_params=pltpu.CompilerParams(
            dimension_semantics=("parallel", "arbitrary"), vmem_limit_bytes=VMEM_LIMIT),
        name="swa_attention",
    )(sinks, p_main, p_tail, p_tail, pt_meta)


def _stack_heads(xb, consts):
    return jnp.concatenate([xb * consts["m_low"], xb * consts["m_high"]], axis=0)


def _stack_quarters(xb, consts):
    return jnp.concatenate([xb * m for m in consts["quarter"]], axis=0)


def _rwkv_const_arrays():
    L, H = CHUNK, CHUNK // 2
    lane = lax.broadcasted_iota(jnp.int32, (L, LANES), 1)
    row_c = lax.broadcasted_iota(jnp.int32, (L, LANES), 0)
    col_c = lane & (HEAD - 1)
    rr = lax.broadcasted_iota(jnp.int32, (LANES, LANES), 0)
    cc = lax.broadcasted_iota(jnp.int32, (LANES, LANES), 1)
    blockdiag = (rr < HEAD) == (cc < HEAD)
    tri2 = row_c >= col_c
    quarter = lane[:H] // H
    second_half = (quarter & 1) == 1
    eye_half = row_c[:H] == (lane[:H] & (H - 1))
    cf = jnp.concatenate([row_c > col_c, row_c >= col_c, blockdiag, second_half, eye_half], axis=0).astype(F32)
    cb = jnp.concatenate([lane < HEAD, lane >= HEAD, tri2, blockdiag], axis=0).astype(BF16)
    cb = jnp.concatenate([cb, cb[3 * L:] * (1.0 / HEAD)] + [(quarter == q).astype(BF16) for q in range(4)], axis=0)
    return cf, cb


def _rwkv_consts(cf_ref, cb_ref):
    L, H = CHUNK, CHUNK // 2
    lane = lax.broadcasted_iota(jnp.int32, (L, LANES), 1)
    base = 3 * L + 2 * LANES
    return dict(
        low=lane < HEAD,
        tri_strict=cf_ref[0:L, :],
        tri_incl=cf_ref[L:2 * L, :],
        blockdiag=cf_ref[2 * L:2 * L + LANES, :],
        second_half=cf_ref[2 * L + LANES:2 * L + LANES + H, :],
        eye_half=cf_ref[2 * L + LANES + H:, :],
        m_low=cb_ref[0:L, :],
        m_high=cb_ref[L:2 * L, :],
        tri2=cb_ref[2 * L:3 * L, :],
        ones_blk=cb_ref[3 * L:3 * L + LANES, :],
        mean_blk=cb_ref[3 * L + LANES:base, :],
        quarter=[cb_ref[base + q * H:base + (q + 1) * H, :] for q in range(4)],
    )


def _unit_lower_inverse(n_mats, consts):
    H = CHUNK // 2
    idx = range(len(n_mats))
    second = consts["second_half"]
    q = consts["quarter"]
    bot2 = [n_mats[i][H:] * second for i in idx]
    diag = [n_mats[i][:H] + bot2[i] for i in idx]
    n21_b = [(n_mats[i][H:] - bot2[i]).astype(BF16) for i in idx]
    d_b = [diag[i].astype(BF16) for i in idx]
    n_pow = [_dot(d_b[i], _stack_quarters(d_b[i], consts)) for i in idx]
    t_d = [consts["eye_half"] + diag[i] for i in idx]
    yield None
    terms = 2
    while terms < H:
        sq = [_stack_quarters(n_pow[i].astype(BF16), consts) for i in idx]
        if 2 * terms < H:
            both = [_dot(jnp.concatenate([t_d[i], n_pow[i]], axis=0).astype(BF16), sq[i]) for i in idx]
            t_d = [t_d[i] + both[i][:H] for i in idx]
            n_pow = [both[i][H:] for i in idx]
        else:
            t_d = [t_d[i] + _dot(t_d[i].astype(BF16), sq[i]) for i in idx]
        terms *= 2
        yield None
    t_db = [t_d[i].astype(BF16) for i in idx]
    m1 = [_dot(n21_b[i], _stack_quarters(t_db[i], consts)).astype(BF16) for i in idx]
    yield None
    zero = jnp.zeros_like(m1[0])
    t21 = [_dot(t_db[i], jnp.concatenate([zero, m1[i] * q[0], zero, m1[i] * q[2]], axis=0)) for i in idx]
    t_low = [t_d[i] * second for i in idx]
    yield [jnp.concatenate([t_d[i] - t_low[i], t21[i] + t_low[i]], axis=0) for i in idx]


def _seg_sum(x, blk, n_pairs):
    L = x.shape[0]
    tiles = jnp.concatenate([x[:, i * LANES:(i + 1) * LANES] for i in range(n_pairs)], axis=0).astype(BF16)
    s = _dot(tiles, blk)
    return jnp.concatenate([s[i * L:(i + 1) * L] for i in range(n_pairs)], axis=1)


def _rwkv_chunk(j, c, refs, hand, consts, n_pairs):
    (r_ref, k_ref, v_ref, wa_ref, mr_ref, mk_ref, mv_ref, mwa_ref, mu_ref, w0_ref, a0_ref, lora_ref,
     kk_ref, ka_ref, rk_ref, lnw_ref, lnb_ref) = refs[:17]
    o_ref, s_ref, carry_ref = refs[-3:]
    nm_ref, lhs_ref, vb_ref, arb_ref, bk_ref, pl_ref, bonus_ref, t_ref = hand
    L = CHUNK
    d = n_pairs * LANES
    idx = range(n_pairs)

    is_meta = c == 0
    z = jnp.concatenate([
        jnp.where(is_meta, mr_ref[...], r_ref[j]),
        jnp.where(is_meta, mk_ref[...], k_ref[j]),
        jnp.where(is_meta, mv_ref[...], v_ref[j]),
        jnp.where(is_meta, mwa_ref[...], wa_ref[j]),
    ], axis=1).astype(F32)
    row = lax.broadcasted_iota(jnp.int32, z.shape, 0)
    z_prev = jnp.where(row == 0, carry_ref[j], pltpu.roll(z, 1, axis=0))
    carry_ref[j] = z[L - 1:L, :]
    z = z + (z_prev - z) * mu_ref[...]
    r_all = z[:, :d]
    k_all = z[:, d:2 * d]
    v_all = z[:, 2 * d:3 * d]
    wa = z[:, 3 * d:]
    yield

    lora_in = jnp.where(consts["low"], jnp.tanh(wa), wa).astype(BF16)
    lora = _dot(lora_in, lora_ref[...])
    lw_all = _sigmoid(w0_ref[...] + lora[:, :d]) * F32(-LOG2_E * math.exp(-0.5))
    a_all = _sigmoid(a0_ref[...] + lora[:, d:])
    hi = lw_all.astype(BF16)
    mid = (lw_all - hi.astype(F32)).astype(BF16)
    cum_all = _dot(consts["tri2"], jnp.concatenate([hi, mid], axis=0))
    yield

    kk = k_all * kk_ref[...]
    kk = kk * lax.rsqrt(jnp.maximum(_seg_sum(kk * kk, consts["ones_blk"], n_pairs), 1e-24))
    k2_all = k_all * (1.0 + (a_all - 1.0) * ka_ref[...])
    av_all = -kk
    bv_all = kk * a_all
    bonus_ref[j] = _seg_sum(r_all * k2_all * rk_ref[...], consts["ones_blk"], n_pairs) * v_all
    yield

    for i in idx:
        sl = slice(i * LANES, (i + 1) * LANES)
        cum = cum_all[:, sl]
        p_incl = jnp.exp2(cum)
        p_excl = jnp.exp2(cum - lw_all[:, sl])
        p_inv = jnp.exp2(-cum)
        a_t = (av_all[:, sl] * p_excl).astype(BF16)
        r_t = (r_all[:, sl] * p_incl).astype(BF16)
        b_t = (bv_all[:, sl] * p_inv).astype(BF16)
        k_t = (k2_all[:, sl] * p_inv).astype(BF16)
        g = _dot_nt(jnp.concatenate([a_t, r_t], axis=0),
                    jnp.concatenate([_stack_heads(b_t, consts), _stack_heads(k_t, consts)], axis=0))
        nm_ref[j, i] = g[:L, :2 * L] * consts["tri_strict"]
        a_ak = (g[:L, 2 * L:] * consts["tri_strict"]).astype(BF16)
        a_rk = (g[L:, 2 * L:] * consts["tri_incl"]).astype(BF16)
        lhs_ref[j, i] = jnp.concatenate([jnp.concatenate([a_t, a_ak], axis=1),
                                         jnp.concatenate([r_t, a_rk], axis=1)], axis=0)
        arb_ref[j, i] = (g[L:, :2 * L] * consts["tri_incl"]).astype(BF16)
        vb_ref[j, i] = v_all[:, sl].astype(BF16)
        bk_ref[j, i] = jnp.concatenate([b_t, k_t], axis=0)
        pl_ref[j, i] = jnp.broadcast_to(p_incl[L - 1:L, :], (8, LANES))
        yield

    t_mats = None
    for t_mats in _unit_lower_inverse([nm_ref[j, i] for i in idx], consts):
        if t_mats is None:
            yield
    for i in idx:
        t_ref[j, i] = t_mats[i].astype(BF16)
    yield

    v_b = [vb_ref[j, i] for i in idx]
    s_prev = [s_ref[j, i] for i in idx]
    xy = [_dot(lhs_ref[j, i], jnp.concatenate([s_prev[i].astype(BF16).T, _stack_heads(v_b[i], consts)], axis=0))
          for i in idx]
    yield
    u_b = [_dot(t_ref[j, i], _stack_heads(xy[i][:L].astype(BF16), consts)).astype(BF16) for i in idx]
    yield
    ys = [xy[i][L:] + _dot(arb_ref[j, i], _stack_heads(u_b[i], consts)) for i in idx]
    for i in idx:
        ds = _dot_tn(jnp.concatenate([u_b[i], v_b[i]], axis=0), bk_ref[j, i])
        s_ref[j, i] = (s_prev[i] + ds * consts["blockdiag"]) * pl_ref[j, i][0:1, :]
    yield

    y_all = jnp.concatenate(ys, axis=1)
    dy = y_all - _seg_sum(y_all, consts["mean_blk"], n_pairs)
    yield
    var = _seg_sum(dy * dy, consts["mean_blk"], n_pairs)
    yn = dy * lax.rsqrt(var + GN_EPS) * lnw_ref[...] + lnb_ref[...]
    o_ref[j] = (yn + bonus_ref[j]).astype(o_ref.dtype)


def _run_staggered(gens, lead):
    done = [False] * len(gens)
    tick = 0
    while not all(done):
        for k, gen in enumerate(gens):
            if not done[k] and tick >= k * lead:
                try:
                    next(gen)
                except StopIteration:
                    done[k] = True
        tick += 1


def _rwkv_kernel(*refs, n_pairs, rows):
    c = pl.program_id(1)
    n_in = 19
    hand = refs[n_in + 1:n_in + 9]
    s_ref, carry_ref = refs[-2:]
    kernel_refs = refs[:n_in + 1] + (s_ref, carry_ref)

    @pl.when(c == 0)
    def _():
        s_ref[...] = jnp.zeros_like(s_ref)
        carry_ref[...] = jnp.zeros_like(carry_ref)

    consts = _rwkv_consts(refs[17], refs[18])
    gens = [_rwkv_chunk(j, c, kernel_refs, hand, consts, n_pairs) for j in range(rows)]
    _run_staggered(gens, RWKV_LEAD)


def _rwkv(p_main, p_tail, pm_meta, pt_meta, params, batch, seq):
    d = p_main.shape[1] // 8
    n_pairs = d // LANES
    nc = seq // CHUNK
    rows = RWKV_ROWS if batch % RWKV_ROWS == 0 else 1
    meta_row = BLOCK // CHUNK - 1
    mu, w0, a0, lora_w, k_k, k_a, r_k, ln_w, ln_b = params
    kern = functools.partial(_rwkv_kernel, n_pairs=n_pairs, rows=rows)
    cf, cb = _rwkv_const_arrays()
    pm3 = p_main.reshape(batch, seq, p_main.shape[1])
    pt3 = p_tail.reshape(batch, seq, p_tail.shape[1])

    def main_spec(col):
        return pl.BlockSpec((rows, CHUNK, d), lambda b, c: (b, jnp.maximum(c - 1, 0), col))

    def meta_spec(col):
        return pl.BlockSpec((CHUNK, d), lambda b, c: (meta_row, col))

    def vec_spec(w):
        return pl.BlockSpec((1, w), lambda b, c: (0, 0))

    out = pl.pallas_call(
        kern,
        grid=(batch // rows, nc + 1),
        in_specs=[
            main_spec(2), main_spec(3), main_spec(4),
            pl.BlockSpec((rows, CHUNK, LANES), lambda b, c: (b, jnp.maximum(c - 1, 0), 2)),
            meta_spec(2), meta_spec(3), meta_spec(4),
            pl.BlockSpec((CHUNK, LANES), lambda b, c: (meta_row, 2)),
            vec_spec(3 * d + LANES), vec_spec(d), vec_spec(d),
            pl.BlockSpec((LANES, 2 * d), lambda b, c: (0, 0)),
            vec_spec(d), vec_spec(d), vec_spec(d), vec_spec(d), vec_spec(d),
            pl.BlockSpec(cf.shape, lambda b, c: (0, 0)), pl.BlockSpec(cb.shape, lambda b, c: (0, 0)),
        ],
        out_specs=pl.BlockSpec((rows, CHUNK, d), lambda b, c: (b, jnp.maximum(c - 1, 0), 0)),
        out_shape=jax.ShapeDtypeStruct((batch, seq, d), BF16),
        scratch_shapes=[
            pltpu.VMEM((rows, n_pairs, CHUNK, LANES), F32),
            pltpu.VMEM((rows, n_pairs, 2 * CHUNK, 2 * LANES), BF16),
            pltpu.VMEM((rows, n_pairs, CHUNK, LANES), BF16),
            pltpu.VMEM((rows, n_pairs, CHUNK, LANES), BF16),
            pltpu.VMEM((rows, n_pairs, 2 * CHUNK, LANES), BF16),
            pltpu.VMEM((rows, n_pairs, 8, LANES), F32),
            pltpu.VMEM((rows, CHUNK, d), F32),
            pltpu.VMEM((rows, n_pairs, CHUNK, LANES), BF16),
            pltpu.VMEM((rows, n_pairs, LANES, LANES), F32),
            pltpu.VMEM((rows, 1, 3 * d + LANES), F32),
        ],
        compiler_params=pltpu.CompilerParams(
            dimension_semantics=("parallel", "arbitrary"), vmem_limit_bytes=VMEM_LIMIT),
        name="rwkv7_chunked",
    )(pm3, pm3, pm3, pt3, pm_meta, pm_meta, pm_meta, pt_meta,
      mu, w0, a0, lora_w, k_k, k_a, r_k, ln_w, ln_b, cf, cb)
    return out.reshape(batch * seq, d)


def _merge_kernel(x_ref, att_ref, ga_ref, rw_ref, gr_ref, ma_ref, mr_ref, wa_ref, wr_ref, wo_ref, fw_ref, o_ref):
    def silu(t):
        return t * jax.nn.sigmoid(t)

    ga = ga_ref[...].astype(F32)
    gr = gr_ref[...].astype(F32)
    ya = _dot((att_ref[...].astype(F32) * silu(ga)).astype(BF16), wa_ref[...])
    yr = _dot((rw_ref[...].astype(F32) * silu(gr)).astype(BF16), wr_ref[...])
    merged = jax.nn.sigmoid(ma_ref[...].astype(F32)) * ya + jax.nn.sigmoid(mr_ref[...].astype(F32)) * yr
    h = x_ref[...] + _dot(merged.astype(BF16), wo_ref[...])
    ms = jnp.mean(h * h, axis=-1, keepdims=True)
    o_ref[...] = h * lax.rsqrt(ms + RMS_EPS) * fw_ref[...]


def _merge(x2d, att, p_main, rw, w_a, w_r, w_o, final_w):
    m, d = x2d.shape
    tm = min(_row_tile(m), 512)

    def rows(col):
        return pl.BlockSpec((tm, d), lambda i: (i, col))

    def full():
        return pl.BlockSpec((d, d), lambda i: (0, 0))

    return pl.pallas_call(
        _merge_kernel,
        grid=(m // tm,),
        in_specs=[rows(0), rows(0), rows(1), rows(0), rows(5), rows(6), rows(7), full(), full(), full(),
                  pl.BlockSpec((1, d), lambda i: (0, 0))],
        out_specs=rows(0),
        out_shape=jax.ShapeDtypeStruct((m, d), F32),
        compiler_params=pltpu.CompilerParams(
            dimension_semantics=("parallel",), vmem_limit_bytes=VMEM_LIMIT),
        name="merge_out",
    )(x2d, att, p_main, rw, p_main, p_main, p_main, w_a, w_r, w_o, final_w)


def _rope_table(pos):
    half = HEAD // 2
    inv = 1.0 / (ROPE_THETA ** (jnp.arange(half, dtype=F32) / half))
    ang = pos[:, None] * inv[None, :]
    cos = jnp.cos(ang)
    sin = jnp.sin(ang)
    cos_pair = jnp.concatenate([cos, cos, cos, cos], axis=1)
    sin_pair = jnp.concatenate([-sin, sin, -sin, sin], axis=1)
    return jnp.concatenate([cos_pair, sin_pair], axis=1)


def kernel(x, meta_tokens, norm_w, w_in, att_sinks, rk_mu, rk_w0, rk_w2, rk_a0, rk_a2, rk_k_k, rk_k_a, rk_r_k,
           rk_ln_w, rk_ln_b, w_branch_att, w_branch_rwkv, w_out, final_norm_w):
    batch, seq, d = x.shape
    assert seq % BLOCK == 0 and d % LANES == 0 and norm_w.shape[0] == 1
    kvw = KV_HEADS * HEAD
    assert kvw == LANES and 2 * LORA == LANES

    w = w_in[0]
    o = 0
    cols = {}
    for name, size in (("q", d), ("ka", kvw), ("va", kvw), ("ga", d), ("r", d), ("kr", d), ("vr", d),
                       ("wa", 2 * LORA), ("gr", d), ("ma", d), ("mr", d)):
        cols[name] = w[:, o:o + size]
        o += size
    w_main = jnp.concatenate([cols[n] for n in ("q", "ga", "r", "kr", "vr", "gr", "ma", "mr")], axis=1).astype(BF16)
    w_tail = jnp.concatenate([cols[n] for n in ("ka", "va", "wa")], axis=1).astype(BF16)

    nw = norm_w[0][None, :]
    x2d = x.reshape(batch * seq, d)
    meta_blk = jnp.concatenate([jnp.zeros((BLOCK - N_META, d), x.dtype), meta_tokens.astype(x.dtype)], axis=0)

    cs_x = _rope_table(N_META + jnp.arange(seq, dtype=F32))
    cs_meta = _rope_table(jnp.maximum(jnp.arange(BLOCK, dtype=F32) - (BLOCK - N_META), 0.0))
    p_main, p_tail = _inproj(x2d, nw, cs_x, w_main, w_tail)
    pm_meta, pt_meta = _inproj(meta_blk, nw, cs_meta, w_main, w_tail)

    att = _attention(att_sinks[0], p_main, p_tail, pt_meta, batch, seq)

    zeros_lora = jnp.zeros((LORA, d), F32)
    lora_w = jnp.concatenate([
        jnp.concatenate([rk_w2[0], zeros_lora], axis=0),
        jnp.concatenate([zeros_lora, rk_a2[0]], axis=0)], axis=1).astype(BF16)
    params = (rk_mu[0][None, :], rk_w0[0][None, :], rk_a0[0][None, :], lora_w, rk_k_k[0][None, :],
              rk_k_a[0][None, :], rk_r_k[0].reshape(1, d), rk_ln_w[0][None, :], rk_ln_b[0][None, :])
    rw = _rwkv(p_main, p_tail, pm_meta, pt_meta, params, batch, seq)

    y = _merge(x2d, att, p_main, rw, w_branch_att[0].astype(BF16), w_branch_rwkv[0].astype(BF16),
               w_out[0].astype(BF16), final_norm_w[None, :])
    return y.reshape(batch, seq, d)
```

```python
import functools
import math

import jax
import jax.numpy as jnp
from jax import lax
from jax.experimental import pallas as pl
from jax.experimental.pallas import tpu as pltpu

F32 = jnp.float32
BF16 = jnp.bfloat16

N_META = 16
BLOCK = 128
HEAD = 64
KV_HEADS = 2
LORA = 64
CHUNK = 64
ROPE_THETA = 10000.0
RMS_EPS = 1e-6
GN_EPS = 64e-5
NEG_INF = -1e30
LOG2_E = 1.4426950408889634
LANES = 128
VMEM_LIMIT = 48 * 1024 * 1024
RWKV_ROWS = 8
RWKV_LEAD = 3

NT_DIMS = (((1,), (1,)), ((), ()))
TN_DIMS = (((0,), (0,)), ((), ()))


def _sigmoid(t):
    return 0.5 * jnp.tanh(0.5 * t) + 0.5


def _dot(a, b):
    return jnp.dot(a, b, preferred_element_type=F32)


def _dot_nt(a, b):
    return lax.dot_general(a, b, NT_DIMS, preferred_element_type=F32)


def _dot_tn(a, b):
    return lax.dot_general(a, b, TN_DIMS, preferred_element_type=F32)


def _row_tile(m):
    for t in (1024, 512, 256, 128):
        if m % t == 0:
            return t
    raise ValueError(f"row count {m} must be a multiple of 128")


def _rope(acc, cs, reps):
    width = acc.shape[1]
    cos = cs[:, :LANES]
    sin = cs[:, LANES:]
    if reps > 1:
        cos = jnp.tile(cos, (1, reps))
        sin = jnp.tile(sin, (1, reps))
    lane = lax.broadcasted_iota(jnp.int32, acc.shape, 1)
    first_half = (lane & (HEAD - 1)) < (HEAD // 2)
    rot = jnp.where(first_half, pltpu.roll(acc, width - HEAD // 2, axis=1), pltpu.roll(acc, HEAD // 2, axis=1))
    return acc * cos + rot * sin


def _inproj_kernel(x_ref, nw_ref, cs_ref, wm_ref, wt_ref, pm_ref, pt_ref, xn_ref, *, n_main):
    j = pl.program_id(1)

    @pl.when(j == 0)
    def _():
        x = x_ref[...]
        ms = jnp.mean(x * x, axis=-1, keepdims=True)
        xn_ref[...] = (x * lax.rsqrt(ms + RMS_EPS) * nw_ref[...]).astype(BF16)
        acc = _dot(xn_ref[...], wm_ref[...])
        pm_ref[...] = _rope(acc, cs_ref[...] * F32(LOG2_E * HEAD ** -0.5), acc.shape[1] // LANES).astype(BF16)

    @pl.when(jnp.logical_and(j > 0, j < n_main))
    def _():
        pm_ref[...] = _dot(xn_ref[...], wm_ref[...]).astype(BF16)

    @pl.when(j == n_main)
    def _():
        acc = _dot(xn_ref[...], wt_ref[...])
        ka = _rope(acc[:, :LANES], cs_ref[...], 1)
        pt_ref[:, :LANES] = ka.astype(BF16)
        pt_ref[:, LANES:] = acc[:, LANES:].astype(BF16)


def _inproj(x2d, norm_w, cs, w_main, w_tail):
    m, d = x2d.shape
    tm = _row_tile(cs.shape[0])
    cs_period_tiles = cs.shape[0] // tm
    n_main = w_main.shape[1] // d
    tail = w_tail.shape[1]
    kern = functools.partial(_inproj_kernel, n_main=n_main)
    return pl.pallas_call(
        kern,
        grid=(m // tm, n_main + 1),
        in_specs=[
            pl.BlockSpec((tm, d), lambda i, j: (i, 0)),
            pl.BlockSpec((1, d), lambda i, j: (0, 0)),
            pl.BlockSpec((tm, 2 * LANES), lambda i, j: (i % cs_period_tiles, 0)),
            pl.BlockSpec((d, d), lambda i, j: (0, jnp.minimum(j, n_main - 1))),
            pl.BlockSpec((d, tail), lambda i, j: (0, 0)),
        ],
        out_specs=[
            pl.BlockSpec((tm, d), lambda i, j: (i, jnp.minimum(j, n_main - 1))),
            pl.BlockSpec((tm, tail), lambda i, j: (i, 0)),
        ],
        out_shape=[
            jax.ShapeDtypeStruct((m, n_main * d), BF16),
            jax.ShapeDtypeStruct((m, tail), BF16),
        ],
        scratch_shapes=[pltpu.VMEM((tm, d), BF16)],
        compiler_params=pltpu.CompilerParams(
            dimension_semantics=("parallel", "arbitrary"), vmem_limit_bytes=VMEM_LIMIT),
        name="inproj",
    )(x2d, norm_w, cs, w_main, w_tail)


def _attn_kernel(sink_ref, q_ref, kvp_ref, kvc_ref, kvm_ref, bias_ref, bias0_ref, o_ref, *, n_pairs, group_pairs):
    n = pl.program_id(1)
    prev = jnp.where(n == 0, kvm_ref[...], kvp_ref[...])
    kv = jnp.concatenate([prev, kvc_ref[...]], axis=0).astype(F32)
    k = kv[:, :LANES]
    v = kv[:, LANES:2 * LANES]
    lane_kv = lax.broadcasted_iota(jnp.int32, k.shape, 1)
    low = lane_kv < HEAD
    k_sw = pltpu.roll(k, HEAD, axis=1)
    v_sw = pltpu.roll(v, HEAD, axis=1)
    k_dup = [jnp.where(low, k, k_sw).astype(BF16), jnp.where(low, k_sw, k).astype(BF16)]
    v_dup = [jnp.where(low, v, v_sw).astype(BF16), jnp.where(low, v_sw, v).astype(BF16)]

    bias = jnp.where(n == 0, bias0_ref[...], bias_ref[...])
    rows = 2 * BLOCK
    top = lax.broadcasted_iota(jnp.int32, (rows, 1), 0) < BLOCK
    lane_q = lax.broadcasted_iota(jnp.int32, (BLOCK, LANES), 1)
    low_q = lane_q < HEAD

    for i in range(n_pairs):
        g = i // group_pairs
        q2 = q_ref[:, i * LANES:(i + 1) * LANES]
        zero = jnp.zeros_like(q2)
        qs = jnp.concatenate([jnp.where(low_q, q2, zero), jnp.where(low_q, zero, q2)], axis=0)
        s = _dot_nt(qs, k_dup[g]) + bias
        sink = jnp.where(top, sink_ref[2 * i], sink_ref[2 * i + 1]) * LOG2_E
        mx = jnp.maximum(jnp.max(s, axis=-1, keepdims=True), sink)
        e = jnp.exp2(s - mx)
        den = jnp.sum(e, axis=-1, keepdims=True) + jnp.exp2(sink - mx)
        o = _dot(e.astype(BF16), v_dup[g]) * (1.0 / den)
        o_ref[:, i * LANES:(i + 1) * LANES] = jnp.where(low_q, o[:BLOCK], o[BLOCK:]).astype(o_ref.dtype)


def _window_bias(first_real):
    rows = 2 * BLOCK
    r = lax.broadcasted_iota(jnp.int32, (rows, rows), 0) & (BLOCK - 1)
    c = lax.broadcasted_iota(jnp.int32, (rows, rows), 1)
    ok = (c > r) & (c <= r + BLOCK) & (c >= first_real)
    return jnp.where(ok, 0.0, NEG_INF).astype(F32)


def _attention(sinks, p_main, p_tail, pt_meta, batch, seq):
    d = p_main.shape[1] // 8
    tail = p_tail.shape[1]
    nbq = seq // BLOCK
    n_pairs = d // LANES
    kern = functools.partial(_attn_kernel, n_pairs=n_pairs, group_pairs=n_pairs // KV_HEADS)
    full = pl.BlockSpec((2 * BLOCK, 2 * BLOCK), lambda b, n: (0, 0))
    return pl.pallas_call(
        kern,
        grid=(batch, nbq),
        in_specs=[
            pl.BlockSpec(memory_space=pltpu.SMEM),
            pl.BlockSpec((BLOCK, d), lambda b, n: (b * nbq + n, 0)),
            pl.BlockSpec((BLOCK, tail), lambda b, n: (b * nbq + jnp.maximum(n - 1, 0), 0)),
            pl.BlockSpec((BLOCK, tail), lambda b, n: (b * nbq + n, 0)),
            pl.BlockSpec((BLOCK, tail), lambda b, n: (0, 0)),
            full, full,
        ],
        out_specs=pl.BlockSpec((BLOCK, d), lambda b, n: (b * nbq + n, 0)),
        out_shape=jax.ShapeDtypeStruct((batch * seq, d), BF16),
        compiler_params=pltpu.CompilerParams(
            dimension_semantics=("parallel", "arbitrary"), vmem_limit_bytes=VMEM_LIMIT),
        name="swa_attention",
    )(sinks, p_main, p_tail, p_tail, pt_meta, _window_bias(0), _window_bias(BLOCK - N_META))


def _stack_heads(xb, consts):
    return jnp.concatenate([xb * consts["m_low"], xb * consts["m_high"]], axis=0)


def _stack_quarters(xb, consts):
    return jnp.concatenate([xb * m for m in consts["quarter"]], axis=0)


def _rwkv_const_arrays():
    L, H = CHUNK, CHUNK // 2
    lane = lax.broadcasted_iota(jnp.int32, (L, LANES), 1)
    row_c = lax.broadcasted_iota(jnp.int32, (L, LANES), 0)
    col_c = lane & (HEAD - 1)
    rr = lax.broadcasted_iota(jnp.int32, (LANES, LANES), 0)
    cc = lax.broadcasted_iota(jnp.int32, (LANES, LANES), 1)
    blockdiag = (rr < HEAD) == (cc < HEAD)
    tri2 = row_c >= col_c
    quarter = lane[:H] // H
    second_half = (quarter & 1) == 1
    eye_half = row_c[:H] == (lane[:H] & (H - 1))
    cf = jnp.concatenate([row_c > col_c, row_c >= col_c, blockdiag, second_half, eye_half], axis=0).astype(F32)
    cb = jnp.concatenate([lane < HEAD, lane >= HEAD, tri2, blockdiag], axis=0).astype(BF16)
    cb = jnp.concatenate([cb, cb[3 * L:] * (1.0 / HEAD)] + [(quarter == q).astype(BF16) for q in range(4)], axis=0)
    return cf, cb


def _rwkv_consts(cf_ref, cb_ref):
    L, H = CHUNK, CHUNK // 2
    lane = lax.broadcasted_iota(jnp.int32, (L, LANES), 1)
    base = 3 * L + 2 * LANES
    return dict(
        low=lane < HEAD,
        tri_strict=cf_ref[0:L, :],
        tri_incl=cf_ref[L:2 * L, :],
        blockdiag=cf_ref[2 * L:2 * L + LANES, :],
        second_half=cf_ref[2 * L + LANES:2 * L + LANES + H, :],
        eye_half=cf_ref[2 * L + LANES + H:, :],
        m_low=cb_ref[0:L, :],
        m_high=cb_ref[L:2 * L, :],
        tri2=cb_ref[2 * L:3 * L, :],
        ones_blk=cb_ref[3 * L:3 * L + LANES, :],
        mean_blk=cb_ref[3 * L + LANES:base, :],
        quarter=[cb_ref[base + q * H:base + (q + 1) * H, :] for q in range(4)],
    )


def _unit_lower_inverse(n_mats, consts):
    H = CHUNK // 2
    idx = range(len(n_mats))
    second = consts["second_half"]
    q = consts["quarter"]
    bot2 = [n_mats[i][H:] * second for i in idx]
    diag = [n_mats[i][:H] + bot2[i] for i in idx]
    n21_b = [(n_mats[i][H:] - bot2[i]).astype(BF16) for i in idx]
    d_b = [diag[i].astype(BF16) for i in idx]
    n_pow = [_dot(d_b[i], _stack_quarters(d_b[i], consts)) for i in idx]
    t_d = [consts["eye_half"] + diag[i] for i in idx]
    yield None
    terms = 2
    while terms < H:
        sq = [_stack_quarters(n_pow[i].astype(BF16), consts) for i in idx]
        if 2 * terms < H:
            both = [_dot(jnp.concatenate([t_d[i], n_pow[i]], axis=0).astype(BF16), sq[i]) for i in idx]
            t_d = [t_d[i] + both[i][:H] for i in idx]
            n_pow = [both[i][H:] for i in idx]
        else:
            t_d = [t_d[i] + _dot(t_d[i].astype(BF16), sq[i]) for i in idx]
        terms *= 2
        yield None
    t_db = [t_d[i].astype(BF16) for i in idx]
    m1 = [_dot(n21_b[i], _stack_quarters(t_db[i], consts)).astype(BF16) for i in idx]
    yield None
    zero = jnp.zeros_like(m1[0])
    t21 = [_dot(t_db[i], jnp.concatenate([zero, m1[i] * q[0], zero, m1[i] * q[2]], axis=0)) for i in idx]
    t_low = [t_d[i] * second for i in idx]
    yield [jnp.concatenate([t_d[i] - t_low[i], t21[i] + t_low[i]], axis=0) for i in idx]


def _seg_sum(x, blk, n_pairs):
    L = x.shape[0]
    tiles = jnp.concatenate([x[:, i * LANES:(i + 1) * LANES] for i in range(n_pairs)], axis=0).astype(BF16)
    s = _dot(tiles, blk)
    return jnp.concatenate([s[i * L:(i + 1) * L] for i in range(n_pairs)], axis=1)


def _rwkv_chunk(j, c, refs, hand, consts, n_pairs):
    (r_ref, k_ref, v_ref, wa_ref, mr_ref, mk_ref, mv_ref, mwa_ref, mu_ref, w0_ref, a0_ref, lora_ref,
     kk_ref, ka_ref, rk_ref, lnw_ref, lnb_ref) = refs[:17]
    o_ref, s_ref, carry_ref = refs[-3:]
    nm_ref, lhs_ref, vb_ref, arb_ref, bk_ref, pl_ref, bonus_ref, t_ref = hand
    L = CHUNK
    d = n_pairs * LANES
    idx = range(n_pairs)

    is_meta = c == 0
    z = jnp.concatenate([
        jnp.where(is_meta, mr_ref[...], r_ref[j]),
        jnp.where(is_meta, mk_ref[...], k_ref[j]),
        jnp.where(is_meta, mv_ref[...], v_ref[j]),
        jnp.where(is_meta, mwa_ref[...], wa_ref[j]),
    ], axis=1).astype(F32)
    row = lax.broadcasted_iota(jnp.int32, z.shape, 0)
    z_prev = jnp.where(row == 0, carry_ref[j], pltpu.roll(z, 1, axis=0))
    carry_ref[j] = z[L - 1:L, :]
    z = z + (z_prev - z) * mu_ref[...]
    r_all = z[:, :d]
    k_all = z[:, d:2 * d]
    v_all = z[:, 2 * d:3 * d]
    wa = z[:, 3 * d:]
    yield

    lora_in = jnp.where(consts["low"], jnp.tanh(wa), wa).astype(BF16)
    lora = _dot(lora_in, lora_ref[...])
    lw_all = _sigmoid(w0_ref[...] + lora[:, :d]) * F32(-LOG2_E * math.exp(-0.5))
    a_all = _sigmoid(a0_ref[...] + lora[:, d:])
    hi = lw_all.astype(BF16)
    mid = (lw_all - hi.astype(F32)).astype(BF16)
    cum_all = _dot(consts["tri2"], jnp.concatenate([hi, mid], axis=0))
    yield

    kk = k_all * kk_ref[...]
    kk = kk * lax.rsqrt(jnp.maximum(_seg_sum(kk * kk, consts["ones_blk"], n_pairs), 1e-24))
    k2_all = k_all * (1.0 + (a_all - 1.0) * ka_ref[...])
    av_all = -kk
    bv_all = kk * a_all
    bonus_ref[j] = _seg_sum(r_all * k2_all * rk_ref[...], consts["ones_blk"], n_pairs) * v_all
    yield

    for i in idx:
        sl = slice(i * LANES, (i + 1) * LANES)
        cum = cum_all[:, sl]
        p_incl = jnp.exp2(cum)
        p_excl = jnp.exp2(cum - lw_all[:, sl])
        p_inv = jnp.exp2(-cum)
        a_t = (av_all[:, sl] * p_excl).astype(BF16)
        r_t = (r_all[:, sl] * p_incl).astype(BF16)
        b_t = (bv_all[:, sl] * p_inv).astype(BF16)
        k_t = (k2_all[:, sl] * p_inv).astype(BF16)
        g = _dot_nt(jnp.concatenate([a_t, r_t], axis=0),
                    jnp.concatenate([_stack_heads(b_t, consts), _stack_heads(k_t, consts)], axis=0))
        nm_ref[j, i] = g[:L, :2 * L] * consts["tri_strict"]
        a_ak = (g[:L, 2 * L:] * consts["tri_strict"]).astype(BF16)
        a_rk = (g[L:, 2 * L:] * consts["tri_incl"]).astype(BF16)
        lhs_ref[j, i] = jnp.concatenate([jnp.concatenate([a_t, a_ak], axis=1),
                                         jnp.concatenate([r_t, a_rk], axis=1)], axis=0)
        arb_ref[j, i] = (g[L:, :2 * L] * consts["tri_incl"]).astype(BF16)
        vb_ref[j, i] = v_all[:, sl].astype(BF16)
        bk_ref[j, i] = jnp.concatenate([b_t, k_t], axis=0)
        pl_ref[j, i] = jnp.broadcast_to(p_incl[L - 1:L, :], (8, LANES))
        yield

    t_mats = None
    for t_mats in _unit_lower_inverse([nm_ref[j, i] for i in idx], consts):
        if t_mats is None:
            yield
    for i in idx:
        t_ref[j, i] = t_mats[i].astype(BF16)
    yield

    v_b = [vb_ref[j, i] for i in idx]
    s_prev = [s_ref[j, i] for i in idx]
    xy = [_dot(lhs_ref[j, i], jnp.concatenate([s_prev[i].astype(BF16).T, _stack_heads(v_b[i], consts)], axis=0))
          for i in idx]
    yield
    u_b = [_dot(t_ref[j, i], _stack_heads(xy[i][:L].astype(BF16), consts)).astype(BF16) for i in idx]
    yield
    ys = [xy[i][L:] + _dot(arb_ref[j, i], _stack_heads(u_b[i], consts)) for i in idx]
    for i in idx:
        ds = _dot_tn(jnp.concatenate([u_b[i], v_b[i]], axis=0), bk_ref[j, i])
        s_ref[j, i] = (s_prev[i] + ds * consts["blockdiag"]) * pl_ref[j, i][0:1, :]
    yield

    y_all = jnp.concatenate(ys, axis=1)
    dy = y_all - _seg_sum(y_all, consts["mean_blk"], n_pairs)
    yield
    var = _seg_sum(dy * dy, consts["mean_blk"], n_pairs)
    yn = dy * lax.rsqrt(var + GN_EPS) * lnw_ref[...] + lnb_ref[...]
    o_ref[j] = (yn + bonus_ref[j]).astype(o_ref.dtype)


def _run_staggered(gens, lead):
    done = [False] * len(gens)
    tick = 0
    while not all(done):
        for k, gen in enumerate(gens):
            if not done[k] and tick >= k * lead:
                try:
                    next(gen)
                except StopIteration:
                    done[k] = True
        tick += 1


def _rwkv_kernel(*refs, n_pairs, rows):
    c = pl.program_id(1)
    n_in = 19
    hand = refs[n_in + 1:n_in + 9]
    s_ref, carry_ref = refs[-2:]
    kernel_refs = refs[:n_in + 1] + (s_ref, carry_ref)

    @pl.when(c == 0)
    def _():
        s_ref[...] = jnp.zeros_like(s_ref)
        carry_ref[...] = jnp.zeros_like(carry_ref)

    consts = _rwkv_consts(refs[17], refs[18])
    gens = [_rwkv_chunk(j, c, kernel_refs, hand, consts, n_pairs) for j in range(rows)]
    _run_staggered(gens, RWKV_LEAD)


def _rwkv(p_main, p_tail, pm_meta, pt_meta, params, batch, seq):
    d = p_main.shape[1] // 8
    n_pairs = d // LANES
    nc = seq // CHUNK
    rows = RWKV_ROWS if batch % RWKV_ROWS == 0 else 1
    meta_row = BLOCK // CHUNK - 1
    mu, w0, a0, lora_w, k_k, k_a, r_k, ln_w, ln_b = params
    kern = functools.partial(_rwkv_kernel, n_pairs=n_pairs, rows=rows)
    cf, cb = _rwkv_const_arrays()
    pm3 = p_main.reshape(batch, seq, p_main.shape[1])
    pt3 = p_tail.reshape(batch, seq, p_tail.shape[1])

    def main_spec(col):
        return pl.BlockSpec((rows, CHUNK, d), lambda b, c: (b, jnp.maximum(c - 1, 0), col))

    def meta_spec(col):
        return pl.BlockSpec((CHUNK, d), lambda b, c: (meta_row, col))

    def vec_spec(w):
        return pl.BlockSpec((1, w), lambda b, c: (0, 0))

    out = pl.pallas_call(
        kern,
        grid=(batch // rows, nc + 1),
        in_specs=[
            main_spec(2), main_spec(3), main_spec(4),
            pl.BlockSpec((rows, CHUNK, LANES), lambda b, c: (b, jnp.maximum(c - 1, 0), 2)),
            meta_spec(2), meta_spec(3), meta_spec(4),
            pl.BlockSpec((CHUNK, LANES), lambda b, c: (meta_row, 2)),
            vec_spec(3 * d + LANES), vec_spec(d), vec_spec(d),
            pl.BlockSpec((LANES, 2 * d), lambda b, c: (0, 0)),
            vec_spec(d), vec_spec(d), vec_spec(d), vec_spec(d), vec_spec(d),
            pl.BlockSpec(cf.shape, lambda b, c: (0, 0)), pl.BlockSpec(cb.shape, lambda b, c: (0, 0)),
        ],
        out_specs=pl.BlockSpec((rows, CHUNK, d), lambda b, c: (b, jnp.maximum(c - 1, 0), 0)),
        out_shape=jax.ShapeDtypeStruct((batch, seq, d), BF16),
        scratch_shapes=[
            pltpu.VMEM((rows, n_pairs, CHUNK, LANES), F32),
            pltpu.VMEM((rows, n_pairs, 2 * CHUNK, 2 * LANES), BF16),
            pltpu.VMEM((rows, n_pairs, CHUNK, LANES), BF16),
            pltpu.VMEM((rows, n_pairs, CHUNK, LANES), BF16),
            pltpu.VMEM((rows, n_pairs, 2 * CHUNK, LANES), BF16),
            pltpu.VMEM((rows, n_pairs, 8, LANES), F32),
            pltpu.VMEM((rows, CHUNK, d), F32),
            pltpu.VMEM((rows, n_pairs, CHUNK, LANES), BF16),
            pltpu.VMEM((rows, n_pairs, LANES, LANES), F32),
            pltpu.VMEM((rows, 1, 3 * d + LANES), F32),
        ],
        compiler_params=pltpu.CompilerParams(
            dimension_semantics=("parallel", "arbitrary"), vmem_limit_bytes=VMEM_LIMIT),
        name="rwkv7_chunked",
    )(pm3, pm3, pm3, pt3, pm_meta, pm_meta, pm_meta, pt_meta,
      mu, w0, a0, lora_w, k_k, k_a, r_k, ln_w, ln_b, cf, cb)
    return out.reshape(batch * seq, d)


def _merge_kernel(x_ref, att_ref, ga_ref, rw_ref, gr_ref, ma_ref, mr_ref, wa_ref, wr_ref, wo_ref, fw_ref, o_ref):
    def gated(val_ref, gate_ref):
        h = gate_ref[...] * 0.5
        return val_ref[...] * (h * (jnp.tanh(h) + 1.0))

    ya = _dot(gated(att_ref, ga_ref), wa_ref[...])
    yr = _dot(gated(rw_ref, gr_ref), wr_ref[...])
    merged = _sigmoid(ma_ref[...].astype(F32)) * ya + _sigmoid(mr_ref[...].astype(F32)) * yr
    h = x_ref[...] + _dot(merged.astype(BF16), wo_ref[...])
    ms = jnp.mean(h * h, axis=-1, keepdims=True)
    o_ref[...] = h * lax.rsqrt(ms + RMS_EPS) * fw_ref[...]


def _merge(x2d, att, p_main, rw, w_a, w_r, w_o, final_w):
    m, d = x2d.shape
    tm = min(_row_tile(m), 512)

    def rows(col):
        return pl.BlockSpec((tm, d), lambda i: (i, col))

    def full():
        return pl.BlockSpec((d, d), lambda i: (0, 0))

    return pl.pallas_call(
        _merge_kernel,
        grid=(m // tm,),
        in_specs=[rows(0), rows(0), rows(1), rows(0), rows(5), rows(6), rows(7), full(), full(), full(),
                  pl.BlockSpec((1, d), lambda i: (0, 0))],
        out_specs=rows(0),
        out_shape=jax.ShapeDtypeStruct((m, d), F32),
        compiler_params=pltpu.CompilerParams(
            dimension_semantics=("parallel",), vmem_limit_bytes=VMEM_LIMIT),
        name="merge_out",
    )(x2d, att, p_main, rw, p_main, p_main, p_main, w_a, w_r, w_o, final_w)


def _rope_table(pos):
    half = HEAD // 2
    inv = 1.0 / (ROPE_THETA ** (jnp.arange(half, dtype=F32) / half))
    ang = pos[:, None] * inv[None, :]
    cos = jnp.cos(ang)
    sin = jnp.sin(ang)
    cos_pair = jnp.concatenate([cos, cos, cos, cos], axis=1)
    sin_pair = jnp.concatenate([-sin, sin, -sin, sin], axis=1)
    return jnp.concatenate([cos_pair, sin_pair], axis=1)


def kernel(x, meta_tokens, norm_w, w_in, att_sinks, rk_mu, rk_w0, rk_w2, rk_a0, rk_a2, rk_k_k, rk_k_a, rk_r_k,
           rk_ln_w, rk_ln_b, w_branch_att, w_branch_rwkv, w_out, final_norm_w):
    batch, seq, d = x.shape
    assert seq % BLOCK == 0 and d % LANES == 0 and norm_w.shape[0] == 1
    kvw = KV_HEADS * HEAD
    assert kvw == LANES and 2 * LORA == LANES

    w = w_in[0]
    o = 0
    cols = {}
    for name, size in (("q", d), ("ka", kvw), ("va", kvw), ("ga", d), ("r", d), ("kr", d), ("vr", d),
                       ("wa", 2 * LORA), ("gr", d), ("ma", d), ("mr", d)):
        cols[name] = w[:, o:o + size]
        o += size
    w_main = jnp.concatenate([cols[n] for n in ("q", "ga", "r", "kr", "vr", "gr", "ma", "mr")], axis=1).astype(BF16)
    w_tail = jnp.concatenate([cols[n] for n in ("ka", "va", "wa")], axis=1).astype(BF16)

    nw = norm_w[0][None, :]
    x2d = x.reshape(batch * seq, d)
    meta_blk = jnp.concatenate([jnp.zeros((BLOCK - N_META, d), x.dtype), meta_tokens.astype(x.dtype)], axis=0)

    cs_x = _rope_table(N_META + jnp.arange(seq, dtype=F32))
    cs_meta = _rope_table(jnp.maximum(jnp.arange(BLOCK, dtype=F32) - (BLOCK - N_META), 0.0))
    p_main, p_tail = _inproj(x2d, nw, cs_x, w_main, w_tail)
    pm_meta, pt_meta = _inproj(meta_blk, nw, cs_meta, w_main, w_tail)

    att = _attention(att_sinks[0], p_main, p_tail, pt_meta, batch, seq)

    zeros_lora = jnp.zeros((LORA, d), F32)
    lora_w = jnp.concatenate([
        jnp.concatenate([rk_w2[0], zeros_lora], axis=0),
        jnp.concatenate([zeros_lora, rk_a2[0]], axis=0)], axis=1).astype(BF16)
    params = (rk_mu[0][None, :], rk_w0[0][None, :], rk_a0[0][None, :], lora_w, rk_k_k[0][None, :],
              rk_k_a[0][None, :], rk_r_k[0].reshape(1, d), rk_ln_w[0][None, :], rk_ln_b[0][None, :])
    rw = _rwkv(p_main, p_tail, pm_meta, pt_meta, params, batch, seq)

    y = _merge(x2d, att, p_main, rw, w_branch_att[0].astype(BF16), w_branch_rwkv[0].astype(BF16),
               w_out[0].astype(BF16), final_norm_w[None, :])
    return y.reshape(batch, seq, d)
```

```python
import functools
import math

import jax
import jax.numpy as jnp
from jax import lax
from jax.experimental import pallas as pl
from jax.experimental.pallas import tpu as pltpu

F32 = jnp.float32
BF16 = jnp.bfloat16

N_META = 16
BLOCK = 128
HEAD = 64
KV_HEADS = 2
LORA = 64
CHUNK = 64
ROPE_THETA = 10000.0
RMS_EPS = 1e-6
GN_EPS = 64e-5
NEG_INF = -1e30
LOG2_E = 1.4426950408889634
LANES = 128
VMEM_LIMIT = 48 * 1024 * 1024
ATTN_GROUP = 4
INPROJ_ROWS = 512
RWKV_ROWS = 8
RWKV_LEAD = 3

NT_DIMS = (((1,), (1,)), ((), ()))
TN_DIMS = (((0,), (0,)), ((), ()))


def _sigmoid(t):
    return 0.5 * jnp.tanh(0.5 * t) + 0.5


def _dot(a, b):
    return jnp.dot(a, b, preferred_element_type=F32)


def _dot_nt(a, b):
    return lax.dot_general(a, b, NT_DIMS, preferred_element_type=F32)


def _dot_tn(a, b):
    return lax.dot_general(a, b, TN_DIMS, preferred_element_type=F32)


def _row_tile(m):
    for t in (1024, 512, 256, 128):
        if m % t == 0:
            return t
    raise ValueError(f"row count {m} must be a multiple of 128")


def _rope(acc, cs, reps):
    width = acc.shape[1]
    cos = cs[:, :LANES]
    sin = cs[:, LANES:]
    if reps > 1:
        cos = jnp.tile(cos, (1, reps))
        sin = jnp.tile(sin, (1, reps))
    lane = lax.broadcasted_iota(jnp.int32, acc.shape, 1)
    first_half = (lane & (HEAD - 1)) < (HEAD // 2)
    rot = jnp.where(first_half, pltpu.roll(acc, width - HEAD // 2, axis=1), pltpu.roll(acc, HEAD // 2, axis=1))
    return acc * cos + rot * sin


def _inproj_kernel(x_ref, nw_ref, cs_ref, wm_ref, wt_ref, pm_ref, pt_ref, *, n_main):
    d = x_ref.shape[1]
    x = x_ref[...]
    ms = jnp.mean(x * x, axis=-1, keepdims=True)
    xn = (x * lax.rsqrt(ms + RMS_EPS) * nw_ref[...]).astype(BF16)
    cs = cs_ref[...]
    for t in range(n_main):
        acc = _dot(xn, wm_ref[:, t * d:(t + 1) * d])
        if t == 0:
            acc = _rope(acc, cs * F32(LOG2_E * HEAD ** -0.5), d // LANES)
        pm_ref[:, t * d:(t + 1) * d] = acc.astype(BF16)
    acc = _dot(xn, wt_ref[...])
    pt_ref[:, :LANES] = _rope(acc[:, :LANES], cs, 1).astype(BF16)
    pt_ref[:, LANES:] = acc[:, LANES:].astype(BF16)


def _inproj(x2d, norm_w, cs, w_main, w_tail):
    m, d = x2d.shape
    tm = min(_row_tile(cs.shape[0]), INPROJ_ROWS)
    cs_period_tiles = cs.shape[0] // tm
    n_main = w_main.shape[1] // d
    tail = w_tail.shape[1]
    kern = functools.partial(_inproj_kernel, n_main=n_main)
    resident = pl.Buffered(1)
    return pl.pallas_call(
        kern,
        grid=(m // tm,),
        in_specs=[
            pl.BlockSpec((tm, d), lambda i: (i, 0)),
            pl.BlockSpec((1, d), lambda i: (0, 0)),
            pl.BlockSpec((tm, 2 * LANES), lambda i: (i % cs_period_tiles, 0)),
            pl.BlockSpec((d, n_main * d), lambda i: (0, 0), pipeline_mode=resident),
            pl.BlockSpec((d, tail), lambda i: (0, 0), pipeline_mode=resident),
        ],
        out_specs=[
            pl.BlockSpec((tm, n_main * d), lambda i: (i, 0)),
            pl.BlockSpec((tm, tail), lambda i: (i, 0)),
        ],
        out_shape=[
            jax.ShapeDtypeStruct((m, n_main * d), BF16),
            jax.ShapeDtypeStruct((m, tail), BF16),
        ],
        compiler_params=pltpu.CompilerParams(
            dimension_semantics=("parallel",), vmem_limit_bytes=VMEM_LIMIT),
        name="inproj",
    )(x2d, norm_w, cs, w_main, w_tail)


def _attn_kernel(sink_ref, q_ref, kvp_ref, kvc_ref, kvm_ref, bias_ref, bias0_ref, o_ref, *, n_pairs, group_pairs):
    n = pl.program_id(1)
    prev = jnp.where(n == 0, kvm_ref[...], kvp_ref[...])
    kv = jnp.concatenate([prev, kvc_ref[...]], axis=0).astype(F32)
    k = kv[:, :LANES]
    v = kv[:, LANES:2 * LANES]
    lane_kv = lax.broadcasted_iota(jnp.int32, k.shape, 1)
    low = lane_kv < HEAD
    k_sw = pltpu.roll(k, HEAD, axis=1)
    v_sw = pltpu.roll(v, HEAD, axis=1)
    k_dup = [jnp.where(low, k, k_sw).astype(BF16), jnp.where(low, k_sw, k).astype(BF16)]
    v_dup = [jnp.where(low, v, v_sw).astype(BF16), jnp.where(low, v_sw, v).astype(BF16)]

    bias = jnp.where(n == 0, bias0_ref[...], bias_ref[...])
    rows = 2 * BLOCK
    top = lax.broadcasted_iota(jnp.int32, (rows, 1), 0) < BLOCK
    lane_q = lax.broadcasted_iota(jnp.int32, (BLOCK, LANES), 1)
    low_q = lane_q < HEAD

    for i0 in range(0, n_pairs, ATTN_GROUP):
        grp = range(i0, min(i0 + ATTN_GROUP, n_pairs))
        s, sink, mx, e, den = {}, {}, {}, {}, {}
        for i in grp:
            q2 = q_ref[:, i * LANES:(i + 1) * LANES]
            zero = jnp.zeros_like(q2)
            qs = jnp.concatenate([jnp.where(low_q, q2, zero), jnp.where(low_q, zero, q2)], axis=0)
            s[i] = _dot_nt(qs, k_dup[i // group_pairs]) + bias
        for i in grp:
            sink[i] = jnp.where(top, sink_ref[2 * i], sink_ref[2 * i + 1]) * LOG2_E
            mx[i] = jnp.maximum(jnp.max(s[i], axis=-1, keepdims=True), sink[i])
        for i in grp:
            e[i] = jnp.exp2(s[i] - mx[i])
            den[i] = jnp.sum(e[i], axis=-1, keepdims=True) + jnp.exp2(sink[i] - mx[i])
        for i in grp:
            o = _dot(e[i].astype(BF16), v_dup[i // group_pairs]) * (1.0 / den[i])
            o_ref[:, i * LANES:(i + 1) * LANES] = jnp.where(low_q, o[:BLOCK], o[BLOCK:]).astype(o_ref.dtype)


def _window_bias(first_real):
    rows = 2 * BLOCK
    r = lax.broadcasted_iota(jnp.int32, (rows, rows), 0) & (BLOCK - 1)
    c = lax.broadcasted_iota(jnp.int32, (rows, rows), 1)
    ok = (c > r) & (c <= r + BLOCK) & (c >= first_real)
    return jnp.where(ok, 0.0, NEG_INF).astype(F32)


def _attention(sinks, p_main, p_tail, pt_meta, batch, seq):
    d = p_main.shape[1] // 8
    tail = p_tail.shape[1]
    nbq = seq // BLOCK
    n_pairs = d // LANES
    kern = functools.partial(_attn_kernel, n_pairs=n_pairs, group_pairs=n_pairs // KV_HEADS)
    full = pl.BlockSpec((2 * BLOCK, 2 * BLOCK), lambda b, n: (0, 0))
    return pl.pallas_call(
        kern,
        grid=(batch, nbq),
        in_specs=[
            pl.BlockSpec(memory_space=pltpu.SMEM),
            pl.BlockSpec((BLOCK, d), lambda b, n: (b * nbq + n, 0)),
            pl.BlockSpec((BLOCK, tail), lambda b, n: (b * nbq + jnp.maximum(n - 1, 0), 0)),
            pl.BlockSpec((BLOCK, tail), lambda b, n: (b * nbq + n, 0)),
            pl.BlockSpec((BLOCK, tail), lambda b, n: (0, 0)),
            full, full,
        ],
        out_specs=pl.BlockSpec((BLOCK, d), lambda b, n: (b * nbq + n, 0)),
        out_shape=jax.ShapeDtypeStruct((batch * seq, d), BF16),
        compiler_params=pltpu.CompilerParams(
            dimension_semantics=("parallel", "arbitrary"), vmem_limit_bytes=VMEM_LIMIT),
        name="swa_attention",
    )(sinks, p_main, p_tail, p_tail, pt_meta, _window_bias(0), _window_bias(BLOCK - N_META))


def _stack_heads(xb, consts):
    return jnp.concatenate([xb * consts["m_low"], xb * consts["m_high"]], axis=0)


def _stack_quarters(xb, consts):
    return jnp.concatenate([xb * m for m in consts["quarter"]], axis=0)


def _rwkv_const_arrays():
    L, H = CHUNK, CHUNK // 2
    lane = lax.broadcasted_iota(jnp.int32, (L, LANES), 1)
    row_c = lax.broadcasted_iota(jnp.int32, (L, LANES), 0)
    col_c = lane & (HEAD - 1)
    rr = lax.broadcasted_iota(jnp.int32, (LANES, LANES), 0)
    cc = lax.broadcasted_iota(jnp.int32, (LANES, LANES), 1)
    blockdiag = (rr < HEAD) == (cc < HEAD)
    tri2 = row_c >= col_c
    quarter = lane[:H] // H
    second_half = (quarter & 1) == 1
    eye_half = row_c[:H] == (lane[:H] & (H - 1))
    cf = jnp.concatenate([row_c > col_c, row_c >= col_c, blockdiag, second_half, eye_half], axis=0).astype(F32)
    cb = jnp.concatenate([lane < HEAD, lane >= HEAD, tri2, blockdiag], axis=0).astype(BF16)
    cb = jnp.concatenate([cb, cb[3 * L:] * (1.0 / HEAD)] + [(quarter == q).astype(BF16) for q in range(4)], axis=0)
    return cf, cb


def _rwkv_consts(cf_ref, cb_ref):
    L, H = CHUNK, CHUNK // 2
    lane = lax.broadcasted_iota(jnp.int32, (L, LANES), 1)
    base = 3 * L + 2 * LANES
    return dict(
        low=lane < HEAD,
        tri_strict=cf_ref[0:L, :],
        tri_incl=cf_ref[L:2 * L, :],
        blockdiag=cf_ref[2 * L:2 * L + LANES, :],
        second_half=cf_ref[2 * L + LANES:2 * L + LANES + H, :],
        eye_half=cf_ref[2 * L + LANES + H:, :],
        m_low=cb_ref[0:L, :],
        m_high=cb_ref[L:2 * L, :],
        tri2=cb_ref[2 * L:3 * L, :],
        ones_blk=cb_ref[3 * L:3 * L + LANES, :],
        mean_blk=cb_ref[3 * L + LANES:base, :],
        quarter=[cb_ref[base + q * H:base + (q + 1) * H, :] for q in range(4)],
    )


def _unit_lower_inverse(n_mats, consts):
    H = CHUNK // 2
    idx = range(len(n_mats))
    second = consts["second_half"]
    q = consts["quarter"]
    bot2 = [n_mats[i][H:] * second for i in idx]
    diag = [n_mats[i][:H] + bot2[i] for i in idx]
    n21_b = [(n_mats[i][H:] - bot2[i]).astype(BF16) for i in idx]
    d_b = [diag[i].astype(BF16) for i in idx]
    n_pow = [_dot(d_b[i], _stack_quarters(d_b[i], consts)) for i in idx]
    t_d = [consts["eye_half"] + diag[i] for i in idx]
    yield None
    terms = 2
    while terms < H:
        sq = [_stack_quarters(n_pow[i].astype(BF16), consts) for i in idx]
        if 2 * terms < H:
            both = [_dot(jnp.concatenate([t_d[i], n_pow[i]], axis=0).astype(BF16), sq[i]) for i in idx]
            t_d = [t_d[i] + both[i][:H] for i in idx]
            n_pow = [both[i][H:] for i in idx]
        else:
            t_d = [t_d[i] + _dot(t_d[i].astype(BF16), sq[i]) for i in idx]
        terms *= 2
        yield None
    t_db = [t_d[i].astype(BF16) for i in idx]
    m1 = [_dot(n21_b[i], _stack_quarters(t_db[i], consts)).astype(BF16) for i in idx]
    yield None
    zero = jnp.zeros_like(m1[0])
    t21 = [_dot(t_db[i], jnp.concatenate([zero, m1[i] * q[0], zero, m1[i] * q[2]], axis=0)) for i in idx]
    t_low = [t_d[i] * second for i in idx]
    yield [jnp.concatenate([t_d[i] - t_low[i], t21[i] + t_low[i]], axis=0) for i in idx]


def _seg_sum(x, blk, n_pairs):
    L = x.shape[0]
    tiles = jnp.concatenate([x[:, i * LANES:(i + 1) * LANES] for i in range(n_pairs)], axis=0).astype(BF16)
    s = _dot(tiles, blk)
    return jnp.concatenate([s[i * L:(i + 1) * L] for i in range(n_pairs)], axis=1)


def _rwkv_chunk(j, c, refs, hand, consts, n_pairs):
    (r_ref, k_ref, v_ref, wa_ref, mr_ref, mk_ref, mv_ref, mwa_ref, mu_ref, w0_ref, a0_ref, lora_ref,
     kk_ref, ka_ref, rk_ref, lnw_ref, lnb_ref) = refs[:17]
    o_ref, s_ref, carry_ref = refs[-3:]
    nm_ref, lhs_ref, vb_ref, arb_ref, bk_ref, pl_ref, bonus_ref, t_ref = hand
    L = CHUNK
    d = n_pairs * LANES
    idx = range(n_pairs)

    is_meta = c == 0
    z = jnp.concatenate([
        jnp.where(is_meta, mr_ref[...], r_ref[j]),
        jnp.where(is_meta, mk_ref[...], k_ref[j]),
        jnp.where(is_meta, mv_ref[...], v_ref[j]),
        jnp.where(is_meta, mwa_ref[...], wa_ref[j]),
    ], axis=1).astype(F32)
    row = lax.broadcasted_iota(jnp.int32, z.shape, 0)
    z_prev = jnp.where(row == 0, carry_ref[j], pltpu.roll(z, 1, axis=0))
    carry_ref[j] = z[L - 1:L, :]
    z = z + (z_prev - z) * mu_ref[...]
    r_all = z[:, :d]
    k_all = z[:, d:2 * d]
    v_all = z[:, 2 * d:3 * d]
    wa = z[:, 3 * d:]
    yield

    lora_in = jnp.where(consts["low"], jnp.tanh(wa), wa).astype(BF16)
    lora = _dot(lora_in, lora_ref[...])
    lw_all = _sigmoid(w0_ref[...] + lora[:, :d]) * F32(-LOG2_E * math.exp(-0.5))
    a_all = _sigmoid(a0_ref[...] + lora[:, d:])
    hi = lw_all.astype(BF16)
    mid = (lw_all - hi.astype(F32)).astype(BF16)
    cum_all = _dot(consts["tri2"], jnp.concatenate([hi, mid], axis=0))
    yield

    kk = k_all * kk_ref[...]
    kk = kk * lax.rsqrt(jnp.maximum(_seg_sum(kk * kk, consts["ones_blk"], n_pairs), 1e-24))
    k2_all = k_all * (1.0 + (a_all - 1.0) * ka_ref[...])
    av_all = -kk
    bv_all = kk * a_all
    bonus_ref[j] = _seg_sum(r_all * k2_all * rk_ref[...], consts["ones_blk"], n_pairs) * v_all
    yield

    for i in idx:
        sl = slice(i * LANES, (i + 1) * LANES)
        cum = cum_all[:, sl]
        p_incl = jnp.exp2(cum)
        p_excl = jnp.exp2(cum - lw_all[:, sl])
        p_inv = jnp.exp2(-cum)
        a_t = (av_all[:, sl] * p_excl).astype(BF16)
        r_t = (r_all[:, sl] * p_incl).astype(BF16)
        b_t = (bv_all[:, sl] * p_inv).astype(BF16)
        k_t = (k2_all[:, sl] * p_inv).astype(BF16)
        g = _dot_nt(jnp.concatenate([a_t, r_t], axis=0),
                    jnp.concatenate([_stack_heads(b_t, consts), _stack_heads(k_t, consts)], axis=0))
        nm_ref[j, i] = g[:L, :2 * L] * consts["tri_strict"]
        a_ak = (g[:L, 2 * L:] * consts["tri_strict"]).astype(BF16)
        a_rk = (g[L:, 2 * L:] * consts["tri_incl"]).astype(BF16)
        lhs_ref[j, i] = jnp.concatenate([jnp.concatenate([a_t, a_ak], axis=1),
                                         jnp.concatenate([r_t, a_rk], axis=1)], axis=0)
        arb_ref[j, i] = (g[L:, :2 * L] * consts["tri_incl"]).astype(BF16)
        vb_ref[j, i] = v_all[:, sl].astype(BF16)
        bk_ref[j, i] = jnp.concatenate([b_t, k_t], axis=0)
        pl_ref[j, i] = jnp.broadcast_to(p_incl[L - 1:L, :], (8, LANES))
        yield

    t_mats = None
    for t_mats in _unit_lower_inverse([nm_ref[j, i] for i in idx], consts):
        if t_mats is None:
            yield
    for i in idx:
        t_ref[j, i] = t_mats[i].astype(BF16)
    yield

    v_b = [vb_ref[j, i] for i in idx]
    s_prev = [s_ref[j, i] for i in idx]
    xy = [_dot(lhs_ref[j, i], jnp.concatenate([s_prev[i].astype(BF16).T, _stack_heads(v_b[i], consts)], axis=0))
          for i in idx]
    yield
    u_b = [_dot(t_ref[j, i], _stack_heads(xy[i][:L].astype(BF16), consts)).astype(BF16) for i in idx]
    yield
    ys = [xy[i][L:] + _dot(arb_ref[j, i], _stack_heads(u_b[i], consts)) for i in idx]
    for i in idx:
        ds = _dot_tn(jnp.concatenate([u_b[i], v_b[i]], axis=0), bk_ref[j, i])
        s_ref[j, i] = (s_prev[i] + ds * consts["blockdiag"]) * pl_ref[j, i][0:1, :]
    yield

    y_all = jnp.concatenate(ys, axis=1)
    dy = y_all - _seg_sum(y_all, consts["mean_blk"], n_pairs)
    yield
    var = _seg_sum(dy * dy, consts["mean_blk"], n_pairs)
    yn = dy * lax.rsqrt(var + GN_EPS) * lnw_ref[...] + lnb_ref[...]
    o_ref[j] = (yn + bonus_ref[j]).astype(o_ref.dtype)


def _run_staggered(gens, lead):
    done = [False] * len(gens)
    tick = 0
    while not all(done):
        for k, gen in enumerate(gens):
            if not done[k] and tick >= k * lead:
                try:
                    next(gen)
                except StopIteration:
                    done[k] = True
        tick += 1


def _rwkv_kernel(*refs, n_pairs, rows):
    c = pl.program_id(1)
    n_in = 19
    hand = refs[n_in + 1:n_in + 9]
    s_ref, carry_ref = refs[-2:]
    kernel_refs = refs[:n_in + 1] + (s_ref, carry_ref)

    @pl.when(c == 0)
    def _():
        s_ref[...] = jnp.zeros_like(s_ref)
        carry_ref[...] = jnp.zeros_like(carry_ref)

    consts = _rwkv_consts(refs[17], refs[18])
    gens = [_rwkv_chunk(j, c, kernel_refs, hand, consts, n_pairs) for j in range(rows)]
    _run_staggered(gens, RWKV_LEAD)


def _rwkv(p_main, p_tail, pm_meta, pt_meta, params, batch, seq):
    d = p_main.shape[1] // 8
    n_pairs = d // LANES
    nc = seq // CHUNK
    rows = RWKV_ROWS if batch % RWKV_ROWS == 0 else 1
    meta_row = BLOCK // CHUNK - 1
    mu, w0, a0, lora_w, k_k, k_a, r_k, ln_w, ln_b = params
    kern = functools.partial(_rwkv_kernel, n_pairs=n_pairs, rows=rows)
    cf, cb = _rwkv_const_arrays()
    pm3 = p_main.reshape(batch, seq, p_main.shape[1])
    pt3 = p_tail.reshape(batch, seq, p_tail.shape[1])

    def main_spec(col):
        return pl.BlockSpec((rows, CHUNK, d), lambda b, c: (b, jnp.maximum(c - 1, 0), col))

    def meta_spec(col):
        return pl.BlockSpec((CHUNK, d), lambda b, c: (meta_row, col))

    def vec_spec(w):
        return pl.BlockSpec((1, w), lambda b, c: (0, 0))

    out = pl.pallas_call(
        kern,
        grid=(batch // rows, nc + 1),
        in_specs=[
            main_spec(2), main_spec(3), main_spec(4),
            pl.BlockSpec((rows, CHUNK, LANES), lambda b, c: (b, jnp.maximum(c - 1, 0), 2)),
            meta_spec(2), meta_spec(3), meta_spec(4),
            pl.BlockSpec((CHUNK, LANES), lambda b, c: (meta_row, 2)),
            vec_spec(3 * d + LANES), vec_spec(d), vec_spec(d),
            pl.BlockSpec((LANES, 2 * d), lambda b, c: (0, 0)),
            vec_spec(d), vec_spec(d), vec_spec(d), vec_spec(d), vec_spec(d),
            pl.BlockSpec(cf.shape, lambda b, c: (0, 0)), pl.BlockSpec(cb.shape, lambda b, c: (0, 0)),
        ],
        out_specs=pl.BlockSpec((rows, CHUNK, d), lambda b, c: (b, jnp.maximum(c - 1, 0), 0)),
        out_shape=jax.ShapeDtypeStruct((batch, seq, d), BF16),
        scratch_shapes=[
            pltpu.VMEM((rows, n_pairs, CHUNK, LANES), F32),
            pltpu.VMEM((rows, n_pairs, 2 * CHUNK, 2 * LANES), BF16),
            pltpu.VMEM((rows, n_pairs, CHUNK, LANES), BF16),
            pltpu.VMEM((rows, n_pairs, CHUNK, LANES), BF16),
            pltpu.VMEM((rows, n_pairs, 2 * CHUNK, LANES), BF16),
            pltpu.VMEM((rows, n_pairs, 8, LANES), F32),
            pltpu.VMEM((rows, CHUNK, d), F32),
            pltpu.VMEM((rows, n_pairs, CHUNK, LANES), BF16),
            pltpu.VMEM((rows, n_pairs, LANES, LANES), F32),
            pltpu.VMEM((rows, 1, 3 * d + LANES), F32),
        ],
        compiler_params=pltpu.CompilerParams(
            dimension_semantics=("parallel", "arbitrary"), vmem_limit_bytes=VMEM_LIMIT),
        name="rwkv7_chunked",
    )(pm3, pm3, pm3, pt3, pm_meta, pm_meta, pm_meta, pt_meta,
      mu, w0, a0, lora_w, k_k, k_a, r_k, ln_w, ln_b, cf, cb)
    return out.reshape(batch * seq, d)


def _merge_kernel(x_ref, att_ref, ga_ref, rw_ref, gr_ref, ma_ref, mr_ref, wa_ref, wr_ref, wo_ref, fw_ref, o_ref):
    def gated(val_ref, gate_ref):
        h = gate_ref[...] * 0.5
        return val_ref[...] * (h * (jnp.tanh(h) + 1.0))

    ya = _dot(gated(att_ref, ga_ref), wa_ref[...])
    yr = _dot(gated(rw_ref, gr_ref), wr_ref[...])
    merged = _sigmoid(ma_ref[...].astype(F32)) * ya + _sigmoid(mr_ref[...].astype(F32)) * yr
    h = x_ref[...] + _dot(merged.astype(BF16), wo_ref[...])
    ms = jnp.mean(h * h, axis=-1, keepdims=True)
    o_ref[...] = h * lax.rsqrt(ms + RMS_EPS) * fw_ref[...]


def _merge(x2d, att, p_main, rw, w_a, w_r, w_o, final_w):
    m, d = x2d.shape
    tm = min(_row_tile(m), 512)

    def rows(col):
        return pl.BlockSpec((tm, d), lambda i: (i, col))

    def full():
        return pl.BlockSpec((d, d), lambda i: (0, 0))

    return pl.pallas_call(
        _merge_kernel,
        grid=(m // tm,),
        in_specs=[rows(0), rows(0), rows(1), rows(0), rows(5), rows(6), rows(7), full(), full(), full(),
                  pl.BlockSpec((1, d), lambda i: (0, 0))],
        out_specs=rows(0),
        out_shape=jax.ShapeDtypeStruct((m, d), F32),
        compiler_params=pltpu.CompilerParams(
            dimension_semantics=("parallel",), vmem_limit_bytes=VMEM_LIMIT),
        name="merge_out",
    )(x2d, att, p_main, rw, p_main, p_main, p_main, w_a, w_r, w_o, final_w)


def _rope_table(pos):
    half = HEAD // 2
    inv = 1.0 / (ROPE_THETA ** (jnp.arange(half, dtype=F32) / half))
    ang = pos[:, None] * inv[None, :]
    cos = jnp.cos(ang)
    sin = jnp.sin(ang)
    cos_pair = jnp.concatenate([cos, cos, cos, cos], axis=1)
    sin_pair = jnp.concatenate([-sin, sin, -sin, sin], axis=1)
    return jnp.concatenate([cos_pair, sin_pair], axis=1)


def kernel(x, meta_tokens, norm_w, w_in, att_sinks, rk_mu, rk_w0, rk_w2, rk_a0, rk_a2, rk_k_k, rk_k_a, rk_r_k,
           rk_ln_w, rk_ln_b, w_branch_att, w_branch_rwkv, w_out, final_norm_w):
    batch, seq, d = x.shape
    assert seq % BLOCK == 0 and d % LANES == 0 and norm_w.shape[0] == 1
    kvw = KV_HEADS * HEAD
    assert kvw == LANES and 2 * LORA == LANES

    w = w_in[0]
    o = 0
    cols = {}
    for name, size in (("q", d), ("ka", kvw), ("va", kvw), ("ga", d), ("r", d), ("kr", d), ("vr", d),
                       ("wa", 2 * LORA), ("gr", d), ("ma", d), ("mr", d)):
        cols[name] = w[:, o:o + size]
        o += size
    w_main = jnp.concatenate([cols[n] for n in ("q", "ga", "r", "kr", "vr", "gr", "ma", "mr")], axis=1).astype(BF16)
    w_tail = jnp.concatenate([cols[n] for n in ("ka", "va", "wa")], axis=1).astype(BF16)

    nw = norm_w[0][None, :]
    x2d = x.reshape(batch * seq, d)
    meta_blk = jnp.concatenate([jnp.zeros((BLOCK - N_META, d), x.dtype), meta_tokens.astype(x.dtype)], axis=0)

    cs_x = _rope_table(N_META + jnp.arange(seq, dtype=F32))
    cs_meta = _rope_table(jnp.maximum(jnp.arange(BLOCK, dtype=F32) - (BLOCK - N_META), 0.0))
    p_main, p_tail = _inproj(x2d, nw, cs_x, w_main, w_tail)
    pm_meta, pt_meta = _inproj(meta_blk, nw, cs_meta, w_main, w_tail)

    att = _attention(att_sinks[0], p_main, p_tail, pt_meta, batch, seq)

    zeros_lora = jnp.zeros((LORA, d), F32)
    lora_w = jnp.concatenate([
        jnp.concatenate([rk_w2[0], zeros_lora], axis=0),
        jnp.concatenate([zeros_lora, rk_a2[0]], axis=0)], axis=1).astype(BF16)
    params = (rk_mu[0][None, :], rk_w0[0][None, :], rk_a0[0][None, :], lora_w, rk_k_k[0][None, :],
              rk_k_a[0][None, :], rk_r_k[0].reshape(1, d), rk_ln_w[0][None, :], rk_ln_b[0][None, :])
    rw = _rwkv(p_main, p_tail, pm_meta, pt_meta, params, batch, seq)

    y = _merge(x2d, att, p_main, rw, w_branch_att[0].astype(BF16), w_branch_rwkv[0].astype(BF16),
               w_out[0].astype(BF16), final_norm_w[None, :])
    return y.reshape(batch, seq, d)
```

```python
import functools
import math

import jax
import jax.numpy as jnp
from jax import lax
from jax.experimental import pallas as pl
from jax.experimental.pallas import tpu as pltpu

F32 = jnp.float32
BF16 = jnp.bfloat16

N_META = 16
BLOCK = 128
HEAD = 64
KV_HEADS = 2
LORA = 64
CHUNK = 64
ROPE_THETA = 10000.0
RMS_EPS = 1e-6
GN_EPS = 64e-5
NEG_INF = -1e30
LOG2_E = 1.4426950408889634
LANES = 128
VMEM_LIMIT = 48 * 1024 * 1024
ATTN_BLOCKS = 4
ATTN_GROUP = 2
INPROJ_ROWS = 512
RWKV_ROWS = 8
RWKV_LEAD = 3

NT_DIMS = (((1,), (1,)), ((), ()))
TN_DIMS = (((0,), (0,)), ((), ()))


def _sigmoid(t):
    return 0.5 * jnp.tanh(0.5 * t) + 0.5


def _dot(a, b):
    return jnp.dot(a, b, preferred_element_type=F32)


def _dot_nt(a, b):
    return lax.dot_general(a, b, NT_DIMS, preferred_element_type=F32)


def _dot_tn(a, b):
    return lax.dot_general(a, b, TN_DIMS, preferred_element_type=F32)


def _row_tile(m):
    for t in (1024, 512, 256, 128):
        if m % t == 0:
            return t
    raise ValueError(f"row count {m} must be a multiple of 128")


def _rope(acc, cs, reps):
    width = acc.shape[1]
    cos = cs[:, :LANES]
    sin = cs[:, LANES:]
    if reps > 1:
        cos = jnp.tile(cos, (1, reps))
        sin = jnp.tile(sin, (1, reps))
    lane = lax.broadcasted_iota(jnp.int32, acc.shape, 1)
    first_half = (lane & (HEAD - 1)) < (HEAD // 2)
    rot = jnp.where(first_half, pltpu.roll(acc, width - HEAD // 2, axis=1), pltpu.roll(acc, HEAD // 2, axis=1))
    return acc * cos + rot * sin


def _inproj_kernel(x_ref, nw_ref, cs_ref, wm_ref, wt_ref, pm_ref, pt_ref, *, n_main):
    d = x_ref.shape[1]
    x = x_ref[...]
    ms = jnp.mean(x * x, axis=-1, keepdims=True)
    xn = (x * lax.rsqrt(ms + RMS_EPS) * nw_ref[...]).astype(BF16)
    cs = cs_ref[...]
    for t in range(n_main):
        acc = _dot(xn, wm_ref[:, t * d:(t + 1) * d])
        if t == 0:
            acc = _rope(acc, cs * F32(LOG2_E * HEAD ** -0.5), d // LANES)
        pm_ref[:, t * d:(t + 1) * d] = acc.astype(BF16)
    acc = _dot(xn, wt_ref[...])
    pt_ref[:, :LANES] = _rope(acc[:, :LANES], cs, 1).astype(BF16)
    pt_ref[:, LANES:] = acc[:, LANES:].astype(BF16)


def _inproj(x2d, norm_w, cs, w_main, w_tail):
    m, d = x2d.shape
    tm = min(_row_tile(cs.shape[0]), INPROJ_ROWS)
    cs_period_tiles = cs.shape[0] // tm
    n_main = w_main.shape[1] // d
    tail = w_tail.shape[1]
    kern = functools.partial(_inproj_kernel, n_main=n_main)
    resident = pl.Buffered(1)
    return pl.pallas_call(
        kern,
        grid=(m // tm,),
        in_specs=[
            pl.BlockSpec((tm, d), lambda i: (i, 0)),
            pl.BlockSpec((1, d), lambda i: (0, 0)),
            pl.BlockSpec((tm, 2 * LANES), lambda i: (i % cs_period_tiles, 0)),
            pl.BlockSpec((d, n_main * d), lambda i: (0, 0), pipeline_mode=resident),
            pl.BlockSpec((d, tail), lambda i: (0, 0), pipeline_mode=resident),
        ],
        out_specs=[
            pl.BlockSpec((tm, n_main * d), lambda i: (i, 0)),
            pl.BlockSpec((tm, tail), lambda i: (i, 0)),
        ],
        out_shape=[
            jax.ShapeDtypeStruct((m, n_main * d), BF16),
            jax.ShapeDtypeStruct((m, tail), BF16),
        ],
        compiler_params=pltpu.CompilerParams(
            dimension_semantics=("parallel",), vmem_limit_bytes=VMEM_LIMIT),
        name="inproj",
    )(x2d, norm_w, cs, w_main, w_tail)


def _attn_kernel(sink_ref, q_ref, kvp_ref, kvc_ref, kvm_ref, bias_ref, bias0_ref, o_ref, *, n_pairs, group_pairs, blocks):
    n = pl.program_id(1)
    prev = jnp.where(n == 0, kvm_ref[...], kvp_ref[...])
    kv = jnp.concatenate([prev, kvc_ref[...]], axis=0).astype(F32)
    k = kv[:, :LANES]
    v = kv[:, LANES:2 * LANES]
    lane_kv = lax.broadcasted_iota(jnp.int32, k.shape, 1)
    low = lane_kv < HEAD
    k_sw = pltpu.roll(k, HEAD, axis=1)
    v_sw = pltpu.roll(v, HEAD, axis=1)
    k_dup = [jnp.where(low, k, k_sw).astype(BF16), jnp.where(low, k_sw, k).astype(BF16)]
    v_dup = [jnp.where(low, v, v_sw).astype(BF16), jnp.where(low, v_sw, v).astype(BF16)]

    bias = [jnp.where(n == 0, bias0_ref[...], bias_ref[...])] + [bias_ref[...]] * (blocks - 1)
    rows = 2 * BLOCK
    top = lax.broadcasted_iota(jnp.int32, (rows, 1), 0) < BLOCK
    lane_q = lax.broadcasted_iota(jnp.int32, (BLOCK, LANES), 1)
    low_q = lane_q < HEAD

    for i0 in range(0, n_pairs, ATTN_GROUP):
        grp = [(t, i) for t in range(blocks) for i in range(i0, min(i0 + ATTN_GROUP, n_pairs))]
        s, sink, mx, e, den = {}, {}, {}, {}, {}
        for t, i in grp:
            q2 = q_ref[t * BLOCK:(t + 1) * BLOCK, i * LANES:(i + 1) * LANES]
            zero = jnp.zeros_like(q2)
            qs = jnp.concatenate([jnp.where(low_q, q2, zero), jnp.where(low_q, zero, q2)], axis=0)
            keys = k_dup[i // group_pairs][t * BLOCK:(t + 2) * BLOCK]
            s[t, i] = _dot_nt(qs, keys) + bias[t]
        for t, i in grp:
            sink[t, i] = jnp.where(top, sink_ref[2 * i], sink_ref[2 * i + 1]) * LOG2_E
            mx[t, i] = jnp.maximum(jnp.max(s[t, i], axis=-1, keepdims=True), sink[t, i])
        for t, i in grp:
            e[t, i] = jnp.exp2(s[t, i] - mx[t, i])
            den[t, i] = jnp.sum(e[t, i], axis=-1, keepdims=True) + jnp.exp2(sink[t, i] - mx[t, i])
        for t, i in grp:
            vals = v_dup[i // group_pairs][t * BLOCK:(t + 2) * BLOCK]
            o = _dot(e[t, i].astype(BF16), vals) * (1.0 / den[t, i])
            o_ref[t * BLOCK:(t + 1) * BLOCK, i * LANES:(i + 1) * LANES] = (
                jnp.where(low_q, o[:BLOCK], o[BLOCK:]).astype(o_ref.dtype))


def _window_bias(first_real):
    rows = 2 * BLOCK
    r = lax.broadcasted_iota(jnp.int32, (rows, rows), 0) & (BLOCK - 1)
    c = lax.broadcasted_iota(jnp.int32, (rows, rows), 1)
    ok = (c > r) & (c <= r + BLOCK) & (c >= first_real)
    return jnp.where(ok, 0.0, NEG_INF).astype(F32)


def _attention(sinks, p_main, p_tail, pt_meta, batch, seq):
    d = p_main.shape[1] // 8
    tail = p_tail.shape[1]
    nbq = seq // BLOCK
    blocks = ATTN_BLOCKS if nbq % ATTN_BLOCKS == 0 else 1
    steps = nbq // blocks
    n_pairs = d // LANES
    kern = functools.partial(_attn_kernel, n_pairs=n_pairs, group_pairs=n_pairs // KV_HEADS, blocks=blocks)
    full = pl.BlockSpec((2 * BLOCK, 2 * BLOCK), lambda b, n: (0, 0))
    return pl.pallas_call(
        kern,
        grid=(batch, steps),
        in_specs=[
            pl.BlockSpec(memory_space=pltpu.SMEM),
            pl.BlockSpec((blocks * BLOCK, d), lambda b, n: (b * steps + n, 0)),
            pl.BlockSpec((BLOCK, tail), lambda b, n: (b * nbq + jnp.maximum(blocks * n - 1, 0), 0)),
            pl.BlockSpec((blocks * BLOCK, tail), lambda b, n: (b * steps + n, 0)),
            pl.BlockSpec((BLOCK, tail), lambda b, n: (0, 0)),
            full, full,
        ],
        out_specs=pl.BlockSpec((blocks * BLOCK, d), lambda b, n: (b * steps + n, 0)),
        out_shape=jax.ShapeDtypeStruct((batch * seq, d), BF16),
        compiler_params=pltpu.CompilerParams(
            dimension_semantics=("parallel", "arbitrary"), vmem_limit_bytes=VMEM_LIMIT),
        name="swa_attention",
    )(sinks, p_main, p_tail, p_tail, pt_meta, _window_bias(0), _window_bias(BLOCK - N_META))


def _stack_heads(xb, consts):
    return jnp.concatenate([xb * consts["m_low"], xb * consts["m_high"]], axis=0)


def _stack_quarters(xb, consts):
    return jnp.concatenate([xb * m for m in consts["quarter"]], axis=0)


def _rwkv_const_arrays():
    L, H = CHUNK, CHUNK // 2
    lane = lax.broadcasted_iota(jnp.int32, (L, LANES), 1)
    row_c = lax.broadcasted_iota(jnp.int32, (L, LANES), 0)
    col_c = lane & (HEAD - 1)
    rr = lax.broadcasted_iota(jnp.int32, (LANES, LANES), 0)
    cc = lax.broadcasted_iota(jnp.int32, (LANES, LANES), 1)
    blockdiag = (rr < HEAD) == (cc < HEAD)
    tri2 = row_c >= col_c
    quarter = lane[:H] // H
    second_half = (quarter & 1) == 1
    eye_half = row_c[:H] == (lane[:H] & (H - 1))
    cf = jnp.concatenate([row_c > col_c, row_c >= col_c, blockdiag, second_half, eye_half], axis=0).astype(F32)
    cb = jnp.concatenate([lane < HEAD, lane >= HEAD, tri2, blockdiag], axis=0).astype(BF16)
    cb = jnp.concatenate([cb, cb[3 * L:] * (1.0 / HEAD)] + [(quarter == q).astype(BF16) for q in range(4)], axis=0)
    return cf, cb


def _rwkv_consts(cf_ref, cb_ref):
    L, H = CHUNK, CHUNK // 2
    lane = lax.broadcasted_iota(jnp.int32, (L, LANES), 1)
    base = 3 * L + 2 * LANES
    return dict(
        low=lane < HEAD,
        tri_strict=cf_ref[0:L, :],
        tri_incl=cf_ref[L:2 * L, :],
        blockdiag=cf_ref[2 * L:2 * L + LANES, :],
        second_half=cf_ref[2 * L + LANES:2 * L + LANES + H, :],
        eye_half=cf_ref[2 * L + LANES + H:, :],
        m_low=cb_ref[0:L, :],
        m_high=cb_ref[L:2 * L, :],
        tri2=cb_ref[2 * L:3 * L, :],
        ones_blk=cb_ref[3 * L:3 * L + LANES, :],
        mean_blk=cb_ref[3 * L + LANES:base, :],
        quarter=[cb_ref[base + q * H:base + (q + 1) * H, :] for q in range(4)],
    )


def _unit_lower_inverse(n_mats, consts):
    H = CHUNK // 2
    idx = range(len(n_mats))
    second = consts["second_half"]
    q = consts["quarter"]
    bot2 = [n_mats[i][H:] * second for i in idx]
    diag = [n_mats[i][:H] + bot2[i] for i in idx]
    n21_b = [(n_mats[i][H:] - bot2[i]).astype(BF16) for i in idx]
    d_b = [diag[i].astype(BF16) for i in idx]
    n_pow = [_dot(d_b[i], _stack_quarters(d_b[i], consts)) for i in idx]
    t_d = [consts["eye_half"] + diag[i] for i in idx]
    yield None
    terms = 2
    while terms < H:
        sq = [_stack_quarters(n_pow[i].astype(BF16), consts) for i in idx]
        if 2 * terms < H:
            both = [_dot(jnp.concatenate([t_d[i], n_pow[i]], axis=0).astype(BF16), sq[i]) for i in idx]
            t_d = [t_d[i] + both[i][:H] for i in idx]
            n_pow = [both[i][H:] for i in idx]
        else:
            t_d = [t_d[i] + _dot(t_d[i].astype(BF16), sq[i]) for i in idx]
        terms *= 2
        yield None
    t_db = [t_d[i].astype(BF16) for i in idx]
    m1 = [_dot(n21_b[i], _stack_quarters(t_db[i], consts)).astype(BF16) for i in idx]
    yield None
    zero = jnp.zeros_like(m1[0])
    t21 = [_dot(t_db[i], jnp.concatenate([zero, m1[i] * q[0], zero, m1[i] * q[2]], axis=0)) for i in idx]
    t_low = [t_d[i] * second for i in idx]
    yield [jnp.concatenate([t_d[i] - t_low[i], t21[i] + t_low[i]], axis=0) for i in idx]


def _seg_sum(x, blk, n_pairs):
    L = x.shape[0]
    tiles = jnp.concatenate([x[:, i * LANES:(i + 1) * LANES] for i in range(n_pairs)], axis=0).astype(BF16)
    s = _dot(tiles, blk)
    return jnp.concatenate([s[i * L:(i + 1) * L] for i in range(n_pairs)], axis=1)


def _rwkv_chunk(j, c, refs, hand, consts, n_pairs):
    (r_ref, k_ref, v_ref, wa_ref, mr_ref, mk_ref, mv_ref, mwa_ref, mu_ref, w0_ref, a0_ref, lora_ref,
     kk_ref, ka_ref, rk_ref, lnw_ref, lnb_ref) = refs[:17]
    o_ref, s_ref, carry_ref = refs[-3:]
    nm_ref, lhs_ref, vb_ref, arb_ref, bk_ref, pl_ref, bonus_ref, t_ref, lora_scr = hand
    L = CHUNK
    d = n_pairs * LANES
    idx = range(n_pairs)

    is_meta = c == 0
    z = jnp.concatenate([
        jnp.where(is_meta, mr_ref[...], r_ref[j]),
        jnp.where(is_meta, mk_ref[...], k_ref[j]),
        jnp.where(is_meta, mv_ref[...], v_ref[j]),
    ], axis=1).astype(F32)
    row = lax.broadcasted_iota(jnp.int32, z.shape, 0)
    z_prev = jnp.where(row == 0, carry_ref[j, :, :3 * d], pltpu.roll(z, 1, axis=0))
    carry_ref[j, :, :3 * d] = z[L - 1:L, :]
    z = z + (z_prev - z) * mu_ref[:, :3 * d]
    r_all = z[:, :d]
    k_all = z[:, d:2 * d]
    v_all = z[:, 2 * d:3 * d]
    yield

    lora = lora_scr[j]
    lw_all = _sigmoid(w0_ref[...] + lora[:, :d]) * F32(-LOG2_E * math.exp(-0.5))
    a_all = _sigmoid(a0_ref[...] + lora[:, d:])
    hi = lw_all.astype(BF16)
    mid = (lw_all - hi.astype(F32)).astype(BF16)
    cum_all = _dot(consts["tri2"], jnp.concatenate([hi, mid], axis=0))
    yield

    kk = k_all * kk_ref[...]
    kk = kk * lax.rsqrt(jnp.maximum(_seg_sum(kk * kk, consts["ones_blk"], n_pairs), 1e-24))
    k2_all = k_all * (1.0 + (a_all - 1.0) * ka_ref[...])
    av_all = -kk
    bv_all = kk * a_all
    bonus_ref[j] = _seg_sum(r_all * k2_all * rk_ref[...], consts["ones_blk"], n_pairs) * v_all
    yield

    for i in idx:
        sl = slice(i * LANES, (i + 1) * LANES)
        cum = cum_all[:, sl]
        p_incl = jnp.exp2(cum)
        p_excl = jnp.exp2(cum - lw_all[:, sl])
        p_inv = jnp.exp2(-cum)
        a_t = (av_all[:, sl] * p_excl).astype(BF16)
        r_t = (r_all[:, sl] * p_incl).astype(BF16)
        b_t = (bv_all[:, sl] * p_inv).astype(BF16)
        k_t = (k2_all[:, sl] * p_inv).astype(BF16)
        g = _dot_nt(jnp.concatenate([a_t, r_t], axis=0),
                    jnp.concatenate([_stack_heads(b_t, consts), _stack_heads(k_t, consts)], axis=0))
        nm_ref[j, i] = g[:L, :2 * L] * consts["tri_strict"]
        a_ak = (g[:L, 2 * L:] * consts["tri_strict"]).astype(BF16)
        a_rk = (g[L:, 2 * L:] * consts["tri_incl"]).astype(BF16)
        lhs_ref[j, i] = jnp.concatenate([jnp.concatenate([a_t, a_ak], axis=1),
                                         jnp.concatenate([r_t, a_rk], axis=1)], axis=0)
        arb_ref[j, i] = (g[L:, :2 * L] * consts["tri_incl"]).astype(BF16)
        vb_ref[j, i] = v_all[:, sl].astype(BF16)
        bk_ref[j, i] = jnp.concatenate([b_t, k_t], axis=0)
        pl_ref[j, i] = jnp.broadcast_to(p_incl[L - 1:L, :], (8, LANES))
        yield

    t_mats = None
    for t_mats in _unit_lower_inverse([nm_ref[j, i] for i in idx], consts):
        if t_mats is None:
            yield
    for i in idx:
        t_ref[j, i] = t_mats[i].astype(BF16)
    yield

    v_b = [vb_ref[j, i] for i in idx]
    s_prev = [s_ref[j, i] for i in idx]
    xy = [_dot(lhs_ref[j, i], jnp.concatenate([s_prev[i].astype(BF16).T, _stack_heads(v_b[i], consts)], axis=0))
          for i in idx]
    yield
    u_b = [_dot(t_ref[j, i], _stack_heads(xy[i][:L].astype(BF16), consts)).astype(BF16) for i in idx]
    yield
    ys = [xy[i][L:] + _dot(arb_ref[j, i], _stack_heads(u_b[i], consts)) for i in idx]
    for i in idx:
        ds = _dot_tn(jnp.concatenate([u_b[i], v_b[i]], axis=0), bk_ref[j, i])
        s_ref[j, i] = (s_prev[i] + ds * consts["blockdiag"]) * pl_ref[j, i][0:1, :]
    yield

    y_all = jnp.concatenate(ys, axis=1)
    dy = y_all - _seg_sum(y_all, consts["mean_blk"], n_pairs)
    yield
    var = _seg_sum(dy * dy, consts["mean_blk"], n_pairs)
    yn = dy * lax.rsqrt(var + GN_EPS) * lnw_ref[...] + lnb_ref[...]
    o_ref[j] = (yn + bonus_ref[j]).astype(o_ref.dtype)


def _run_staggered(gens, lead):
    done = [False] * len(gens)
    tick = 0
    while not all(done):
        for k, gen in enumerate(gens):
            if not done[k] and tick >= k * lead:
                try:
                    next(gen)
                except StopIteration:
                    done[k] = True
        tick += 1


def _rwkv_kernel(*refs, n_pairs, rows):
    c = pl.program_id(1)
    n_in = 19
    hand = refs[n_in + 1:n_in + 10]
    s_ref, carry_ref = refs[-2:]
    kernel_refs = refs[:n_in + 1] + (s_ref, carry_ref)

    @pl.when(c == 0)
    def _():
        s_ref[...] = jnp.zeros_like(s_ref)
        carry_ref[...] = jnp.zeros_like(carry_ref)

    consts = _rwkv_consts(refs[17], refs[18])

    wa_ref, mwa_ref, mu_ref, lora_ref = refs[3], refs[7], refs[8], refs[11]
    lora_scr = hand[8]
    d = n_pairs * LANES
    L = CHUNK
    row = lax.broadcasted_iota(jnp.int32, (L, LANES), 0)
    lora_in = []
    for j in range(rows):
        wa = jnp.where(c == 0, mwa_ref[...], wa_ref[j]).astype(F32)
        wa_prev = jnp.where(row == 0, carry_ref[j, :, 3 * d:], pltpu.roll(wa, 1, axis=0))
        carry_ref[j, :, 3 * d:] = wa[L - 1:L, :]
        wa = wa + (wa_prev - wa) * mu_ref[:, 3 * d:]
        lora_in.append(jnp.where(consts["low"], jnp.tanh(wa), wa).astype(BF16))
    lora_all = _dot(jnp.concatenate(lora_in, axis=0), lora_ref[...])
    for j in range(rows):
        lora_scr[j] = lora_all[j * L:(j + 1) * L]

    gens =[_rwkv_chunk(j, c, kernel_refs, hand, consts, n_pairs) for j in range(rows)]
    _run_staggered(gens, RWKV_LEAD)


def _rwkv(p_main, p_tail, pm_meta, pt_meta, params, batch, seq):
    d = p_main.shape[1] // 8
    n_pairs = d // LANES
    nc = seq // CHUNK
    rows = RWKV_ROWS if batch % RWKV_ROWS == 0 else 1
    meta_row = BLOCK // CHUNK - 1
    mu, w0, a0, lora_w, k_k, k_a, r_k, ln_w, ln_b = params
    kern = functools.partial(_rwkv_kernel, n_pairs=n_pairs, rows=rows)
    cf, cb = _rwkv_const_arrays()
    pm3 = p_main.reshape(batch, seq, p_main.shape[1])
    pt3 = p_tail.reshape(batch, seq, p_tail.shape[1])

    def main_spec(col):
        return pl.BlockSpec((rows, CHUNK, d), lambda b, c: (b, jnp.maximum(c - 1, 0), col))

    def meta_spec(col):
        return pl.BlockSpec((CHUNK, d), lambda b, c: (meta_row, col))

    def vec_spec(w):
        return pl.BlockSpec((1, w), lambda b, c: (0, 0))

    out = pl.pallas_call(
        kern,
        grid=(batch // rows, nc + 1),
        in_specs=[
            main_spec(2), main_spec(3), main_spec(4),
            pl.BlockSpec((rows, CHUNK, LANES), lambda b, c: (b, jnp.maximum(c - 1, 0), 2)),
            meta_spec(2), meta_spec(3), meta_spec(4),
            pl.BlockSpec((CHUNK, LANES), lambda b, c: (meta_row, 2)),
            vec_spec(3 * d + LANES), vec_spec(d), vec_spec(d),
            pl.BlockSpec((LANES, 2 * d), lambda b, c: (0, 0)),
            vec_spec(d), vec_spec(d), vec_spec(d), vec_spec(d), vec_spec(d),
            pl.BlockSpec(cf.shape, lambda b, c: (0, 0)), pl.BlockSpec(cb.shape, lambda b, c: (0, 0)),
        ],
        out_specs=pl.BlockSpec((rows, CHUNK, d), lambda b, c: (b, jnp.maximum(c - 1, 0), 0)),
        out_shape=jax.ShapeDtypeStruct((batch, seq, d), BF16),
        scratch_shapes=[
            pltpu.VMEM((rows, n_pairs, CHUNK, LANES), F32),
            pltpu.VMEM((rows, n_pairs, 2 * CHUNK, 2 * LANES), BF16),
            pltpu.VMEM((rows, n_pairs, CHUNK, LANES), BF16),
            pltpu.VMEM((rows, n_pairs, CHUNK, LANES), BF16),
            pltpu.VMEM((rows, n_pairs, 2 * CHUNK, LANES), BF16),
            pltpu.VMEM((rows, n_pairs, 8, LANES), F32),
            pltpu.VMEM((rows, CHUNK, d), F32),
            pltpu.VMEM((rows, n_pairs, CHUNK, LANES), BF16),
            pltpu.VMEM((rows, CHUNK, 2 * d), F32),
            pltpu.VMEM((rows, n_pairs, LANES, LANES), F32),
            pltpu.VMEM((rows, 1, 3 * d + LANES), F32),
        ],
        compiler_params=pltpu.CompilerParams(
            dimension_semantics=("parallel", "arbitrary"), vmem_limit_bytes=VMEM_LIMIT),
        name="rwkv7_chunked",
    )(pm3, pm3, pm3, pt3, pm_meta, pm_meta, pm_meta, pt_meta,
      mu, w0, a0, lora_w, k_k, k_a, r_k, ln_w, ln_b, cf, cb)
    return out.reshape(batch * seq, d)


def _merge_kernel(x_ref, att_ref, ga_ref, rw_ref, gr_ref, ma_ref, mr_ref, wa_ref, wr_ref, wo_ref, fw_ref, o_ref):
    def gated(val_ref, gate_ref):
        h = gate_ref[...] * 0.5
        return val_ref[...] * (h * (jnp.tanh(h) + 1.0))

    ya = _dot(gated(att_ref, ga_ref), wa_ref[...])
    yr = _dot(gated(rw_ref, gr_ref), wr_ref[...])
    merged = _sigmoid(ma_ref[...].astype(F32)) * ya + _sigmoid(mr_ref[...].astype(F32)) * yr
    h = x_ref[...] + _dot(merged.astype(BF16), wo_ref[...])
    ms = jnp.mean(h * h, axis=-1, keepdims=True)
    o_ref[...] = h * lax.rsqrt(ms + RMS_EPS) * fw_ref[...]


def _merge(x2d, att, p_main, rw, w_a, w_r, w_o, final_w):
    m, d = x2d.shape
    tm = min(_row_tile(m), 512)

    def rows(col):
        return pl.BlockSpec((tm, d), lambda i: (i, col))

    def full():
        return pl.BlockSpec((d, d), lambda i: (0, 0))

    return pl.pallas_call(
        _merge_kernel,
        grid=(m // tm,),
        in_specs=[rows(0), rows(0), rows(1), rows(0), rows(5), rows(6), rows(7), full(), full(), full(),
                  pl.BlockSpec((1, d), lambda i: (0, 0))],
        out_specs=rows(0),
        out_shape=jax.ShapeDtypeStruct((m, d), F32),
        compiler_params=pltpu.CompilerParams(
            dimension_semantics=("parallel",), vmem_limit_bytes=VMEM_LIMIT),
        name="merge_out",
    )(x2d, att, p_main, rw, p_main, p_main, p_main, w_a, w_r, w_o, final_w)


def _rope_table(pos):
    half = HEAD // 2
    inv = 1.0 / (ROPE_THETA ** (jnp.arange(half, dtype=F32) / half))
    ang = pos[:, None] * inv[None, :]
    cos = jnp.cos(ang)
    sin = jnp.sin(ang)
    cos_pair = jnp.concatenate([cos, cos, cos, cos], axis=1)
    sin_pair = jnp.concatenate([-sin, sin, -sin, sin], axis=1)
    return jnp.concatenate([cos_pair, sin_pair], axis=1)


def kernel(x, meta_tokens, norm_w, w_in, att_sinks, rk_mu, rk_w0, rk_w2, rk_a0, rk_a2, rk_k_k, rk_k_a, rk_r_k,
           rk_ln_w, rk_ln_b, w_branch_att, w_branch_rwkv, w_out, final_norm_w):
    batch, seq, d = x.shape
    assert seq % BLOCK == 0 and d % LANES == 0 and norm_w.shape[0] == 1
    kvw = KV_HEADS * HEAD
    assert kvw == LANES and 2 * LORA == LANES

    w = w_in[0]
    o = 0
    cols = {}
    for name, size in (("q", d), ("ka", kvw), ("va", kvw), ("ga", d), ("r", d), ("kr", d), ("vr", d),
                       ("wa", 2 * LORA), ("gr", d), ("ma", d), ("mr", d)):
        cols[name] = w[:, o:o + size]
        o += size
    w_main = jnp.concatenate([cols[n] for n in ("q", "ga", "r", "kr", "vr", "gr", "ma", "mr")], axis=1).astype(BF16)
    w_tail = jnp.concatenate([cols[n] for n in ("ka", "va", "wa")], axis=1).astype(BF16)

    nw = norm_w[0][None, :]
    x2d = x.reshape(batch * seq, d)
    meta_blk = jnp.concatenate([jnp.zeros((BLOCK - N_META, d), x.dtype), meta_tokens.astype(x.dtype)], axis=0)

    cs_x = _rope_table(N_META + jnp.arange(seq, dtype=F32))
    cs_meta = _rope_table(jnp.maximum(jnp.arange(BLOCK, dtype=F32) - (BLOCK - N_META), 0.0))
    p_main, p_tail = _inproj(x2d, nw, cs_x, w_main, w_tail)
    pm_meta, pt_meta = _inproj(meta_blk, nw, cs_meta, w_main, w_tail)

    att = _attention(att_sinks[0], p_main, p_tail, pt_meta, batch, seq)

    zeros_lora = jnp.zeros((LORA, d), F32)
    lora_w = jnp.concatenate([
        jnp.concatenate([rk_w2[0], zeros_lora], axis=0),
        jnp.concatenate([zeros_lora, rk_a2[0]], axis=0)], axis=1).astype(BF16)
    params = (rk_mu[0][None, :], rk_w0[0][None, :], rk_a0[0][None, :], lora_w, rk_k_k[0][None, :],
              rk_k_a[0][None, :], rk_r_k[0].reshape(1, d), rk_ln_w[0][None, :], rk_ln_b[0][None, :])
    rw = _rwkv(p_main, p_tail, pm_meta, pt_meta, params, batch, seq)

    y = _merge(x2d, att, p_main, rw, w_branch_att[0].astype(BF16), w_branch_rwkv[0].astype(BF16),
               w_out[0].astype(BF16), final_norm_w[None, :])
    return y.reshape(batch, seq, d)
```

```python
import functools
import math

import jax
import jax.numpy as jnp
from jax import lax
from jax.experimental import pallas as pl
from jax.experimental.pallas import tpu as pltpu

F32 = jnp.float32
BF16 = jnp.bfloat16

N_META = 16
BLOCK = 128
HEAD = 64
KV_HEADS = 2
LORA = 64
CHUNK = 64
ROPE_THETA = 10000.0
RMS_EPS = 1e-6
GN_EPS = 64e-5
NEG_INF = -1e30
LOG2_E = 1.4426950408889634
LANES = 128
VMEM_LIMIT = 48 * 1024 * 1024
ATTN_BLOCKS = 4
ATTN_GROUP = 2
INPROJ_ROWS = 512
RWKV_ROWS = 8
RWKV_LEAD = 3

NT_DIMS = (((1,), (1,)), ((), ()))
TN_DIMS = (((0,), (0,)), ((), ()))


def _sigmoid(t):
    return 0.5 * jnp.tanh(0.5 * t) + 0.5


def _dot(a, b):
    return jnp.dot(a, b, preferred_element_type=F32)


def _dot_nt(a, b):
    return lax.dot_general(a, b, NT_DIMS, preferred_element_type=F32)


def _dot_tn(a, b):
    return lax.dot_general(a, b, TN_DIMS, preferred_element_type=F32)


def _row_tile(m):
    for t in (1024, 512, 256, 128):
        if m % t == 0:
            return t
    raise ValueError(f"row count {m} must be a multiple of 128")


def _rope(acc, cs, reps):
    width = acc.shape[1]
    cos = cs[:, :LANES]
    sin = cs[:, LANES:]
    if reps > 1:
        cos = jnp.tile(cos, (1, reps))
        sin = jnp.tile(sin, (1, reps))
    lane = lax.broadcasted_iota(jnp.int32, acc.shape, 1)
    first_half = (lane & (HEAD - 1)) < (HEAD // 2)
    rot = jnp.where(first_half, pltpu.roll(acc, width - HEAD // 2, axis=1), pltpu.roll(acc, HEAD // 2, axis=1))
    return acc * cos + rot * sin


def _inproj_kernel(x_ref, nw_ref, cs_ref, w_ref, pm_ref, pt_ref, *, main_cols, kv_col, wa_col):
    d = x_ref.shape[1]
    x = x_ref[...]
    ms = jnp.mean(x * x, axis=-1, keepdims=True)
    xn = (x * lax.rsqrt(ms + RMS_EPS) * nw_ref[...]).astype(BF16)
    cs = cs_ref[...]
    for t, col in enumerate(main_cols):
        acc = _dot(xn, w_ref[:, col:col + d])
        if t == 0:
            acc = _rope(acc, cs * F32(LOG2_E * HEAD ** -0.5), d // LANES)
        pm_ref[:, t * d:(t + 1) * d] = acc.astype(BF16)
    acc = _dot(xn, w_ref[:, kv_col:kv_col + 2 * LANES])
    pt_ref[:, :LANES] = _rope(acc[:, :LANES], cs, 1).astype(BF16)
    pt_ref[:, LANES:2 * LANES] = acc[:, LANES:].astype(BF16)
    pt_ref[:, 2 * LANES:] = _dot(xn, w_ref[:, wa_col:wa_col + LANES]).astype(BF16)


def _inproj(x2d, norm_w, cs, w_bf, main_cols, kv_col, wa_col):
    m, d = x2d.shape
    tm = min(_row_tile(cs.shape[0]), INPROJ_ROWS)
    cs_period_tiles = cs.shape[0] // tm
    n_main = len(main_cols)
    tail = 3 * LANES
    kern = functools.partial(_inproj_kernel, main_cols=main_cols, kv_col=kv_col, wa_col=wa_col)
    resident = pl.Buffered(1)
    return pl.pallas_call(
        kern,
        grid=(m // tm,),
        in_specs=[
            pl.BlockSpec((tm, d), lambda i: (i, 0)),
            pl.BlockSpec((1, d), lambda i: (0, 0)),
            pl.BlockSpec((tm, 2 * LANES), lambda i: (i % cs_period_tiles, 0)),
            pl.BlockSpec(w_bf.shape, lambda i: (0, 0), pipeline_mode=resident),
        ],
        out_specs=[
            pl.BlockSpec((tm, n_main * d), lambda i: (i, 0)),
            pl.BlockSpec((tm, tail), lambda i: (i, 0)),
        ],
        out_shape=[
            jax.ShapeDtypeStruct((m, n_main * d), BF16),
            jax.ShapeDtypeStruct((m, tail), BF16),
        ],
        compiler_params=pltpu.CompilerParams(
            dimension_semantics=("parallel",), vmem_limit_bytes=VMEM_LIMIT),
        name="inproj",
    )(x2d, norm_w, cs, w_bf)


def _attn_kernel(sink_ref, q_ref, kvp_ref, kvc_ref, kvm_ref, bias_ref, bias0_ref, o_ref, *, n_pairs, group_pairs, blocks):
    n = pl.program_id(1)
    prev = jnp.where(n == 0, kvm_ref[...], kvp_ref[...])
    kv = jnp.concatenate([prev, kvc_ref[...]], axis=0).astype(F32)
    k = kv[:, :LANES]
    v = kv[:, LANES:2 * LANES]
    lane_kv = lax.broadcasted_iota(jnp.int32, k.shape, 1)
    low = lane_kv < HEAD
    k_sw = pltpu.roll(k, HEAD, axis=1)
    v_sw = pltpu.roll(v, HEAD, axis=1)
    k_dup = [jnp.where(low, k, k_sw).astype(BF16), jnp.where(low, k_sw, k).astype(BF16)]
    v_dup = [jnp.where(low, v, v_sw).astype(BF16), jnp.where(low, v_sw, v).astype(BF16)]

    bias = [jnp.where(n == 0, bias0_ref[...], bias_ref[...])] + [bias_ref[...]] * (blocks - 1)
    rows = 2 * BLOCK
    top = lax.broadcasted_iota(jnp.int32, (rows, 1), 0) < BLOCK
    lane_q = lax.broadcasted_iota(jnp.int32, (BLOCK, LANES), 1)
    low_q = lane_q < HEAD

    for i0 in range(0, n_pairs, ATTN_GROUP):
        grp = [(t, i) for t in range(blocks) for i in range(i0, min(i0 + ATTN_GROUP, n_pairs))]
        s, sink, mx, e, den = {}, {}, {}, {}, {}
        for t, i in grp:
            q2 = q_ref[t * BLOCK:(t + 1) * BLOCK, i * LANES:(i + 1) * LANES]
            zero = jnp.zeros_like(q2)
            qs = jnp.concatenate([jnp.where(low_q, q2, zero), jnp.where(low_q, zero, q2)], axis=0)
            keys = k_dup[i // group_pairs][t * BLOCK:(t + 2) * BLOCK]
            s[t, i] = _dot_nt(qs, keys) + bias[t]
        for t, i in grp:
            sink[t, i] = jnp.where(top, sink_ref[2 * i], sink_ref[2 * i + 1]) * LOG2_E
            mx[t, i] = jnp.maximum(jnp.max(s[t, i], axis=-1, keepdims=True), sink[t, i])
        for t, i in grp:
            e[t, i] = jnp.exp2(s[t, i] - mx[t, i])
            den[t, i] = jnp.sum(e[t, i], axis=-1, keepdims=True) + jnp.exp2(sink[t, i] - mx[t, i])
        for t, i in grp:
            vals = v_dup[i // group_pairs][t * BLOCK:(t + 2) * BLOCK]
            o = _dot(e[t, i].astype(BF16), vals) * (1.0 / den[t, i])
            o_ref[t * BLOCK:(t + 1) * BLOCK, i * LANES:(i + 1) * LANES] = (
                jnp.where(low_q, o[:BLOCK], o[BLOCK:]).astype(o_ref.dtype))


def _window_bias(first_real):
    rows = 2 * BLOCK
    r = lax.broadcasted_iota(jnp.int32, (rows, rows), 0) & (BLOCK - 1)
    c = lax.broadcasted_iota(jnp.int32, (rows, rows), 1)
    ok = (c > r) & (c <= r + BLOCK) & (c >= first_real)
    return jnp.where(ok, 0.0, NEG_INF).astype(F32)


def _attention(sinks, p_main, p_tail, pt_meta, batch, seq):
    d = p_main.shape[1] // 8
    tail = p_tail.shape[1]
    nbq = seq // BLOCK
    blocks = ATTN_BLOCKS if nbq % ATTN_BLOCKS == 0 else 1
    steps = nbq // blocks
    n_pairs = d // LANES
    kern = functools.partial(_attn_kernel, n_pairs=n_pairs, group_pairs=n_pairs // KV_HEADS, blocks=blocks)
    full = pl.BlockSpec((2 * BLOCK, 2 * BLOCK), lambda b, n: (0, 0))
    return pl.pallas_call(
        kern,
        grid=(batch, steps),
        in_specs=[
            pl.BlockSpec(memory_space=pltpu.SMEM),
            pl.BlockSpec((blocks * BLOCK, d), lambda b, n: (b * steps + n, 0)),
            pl.BlockSpec((BLOCK, tail), lambda b, n: (b * nbq + jnp.maximum(blocks * n - 1, 0), 0)),
            pl.BlockSpec((blocks * BLOCK, tail), lambda b, n: (b * steps + n, 0)),
            pl.BlockSpec((BLOCK, tail), lambda b, n: (0, 0)),
            full, full,
        ],
        out_specs=pl.BlockSpec((blocks * BLOCK, d), lambda b, n: (b * steps + n, 0)),
        out_shape=jax.ShapeDtypeStruct((batch * seq, d), BF16),
        compiler_params=pltpu.CompilerParams(
            dimension_semantics=("parallel", "arbitrary"), vmem_limit_bytes=VMEM_LIMIT),
        name="swa_attention",
    )(sinks, p_main, p_tail, p_tail, pt_meta, _window_bias(0), _window_bias(BLOCK - N_META))


def _stack_heads(xb, consts):
    return jnp.concatenate([xb * consts["m_low"], xb * consts["m_high"]], axis=0)


def _stack_quarters(xb, consts):
    return jnp.concatenate([xb * m for m in consts["quarter"]], axis=0)


def _rwkv_const_arrays():
    L, H = CHUNK, CHUNK // 2
    lane = lax.broadcasted_iota(jnp.int32, (L, LANES), 1)
    row_c = lax.broadcasted_iota(jnp.int32, (L, LANES), 0)
    col_c = lane & (HEAD - 1)
    rr = lax.broadcasted_iota(jnp.int32, (LANES, LANES), 0)
    cc = lax.broadcasted_iota(jnp.int32, (LANES, LANES), 1)
    blockdiag = (rr < HEAD) == (cc < HEAD)
    tri2 = row_c >= col_c
    quarter = lane[:H] // H
    second_half = (quarter & 1) == 1
    eye_half = row_c[:H] == (lane[:H] & (H - 1))
    cf = jnp.concatenate([row_c > col_c, row_c >= col_c, blockdiag, second_half, eye_half], axis=0).astype(F32)
    cb = jnp.concatenate([lane < HEAD, lane >= HEAD, tri2, blockdiag], axis=0).astype(BF16)
    cb = jnp.concatenate([cb, cb[3 * L:] * (1.0 / HEAD)] + [(quarter == q).astype(BF16) for q in range(4)], axis=0)
    return cf, cb


def _rwkv_consts(cf_ref, cb_ref):
    L, H = CHUNK, CHUNK // 2
    lane = lax.broadcasted_iota(jnp.int32, (L, LANES), 1)
    base = 3 * L + 2 * LANES
    return dict(
        low=lane < HEAD,
        tri_strict=cf_ref[0:L, :],
        tri_incl=cf_ref[L:2 * L, :],
        blockdiag=cf_ref[2 * L:2 * L + LANES, :],
        second_half=cf_ref[2 * L + LANES:2 * L + LANES + H, :],
        eye_half=cf_ref[2 * L + LANES + H:, :],
        m_low=cb_ref[0:L, :],
        m_high=cb_ref[L:2 * L, :],
        tri2=cb_ref[2 * L:3 * L, :],
        ones_blk=cb_ref[3 * L:3 * L + LANES, :],
        mean_blk=cb_ref[3 * L + LANES:base, :],
        quarter=[cb_ref[base + q * H:base + (q + 1) * H, :] for q in range(4)],
    )


def _unit_lower_inverse(n_mats, consts):
    H = CHUNK // 2
    idx = range(len(n_mats))
    second = consts["second_half"]
    q = consts["quarter"]
    bot2 = [n_mats[i][H:] * second for i in idx]
    diag = [n_mats[i][:H] + bot2[i] for i in idx]
    n21_b = [(n_mats[i][H:] - bot2[i]).astype(BF16) for i in idx]
    d_b = [diag[i].astype(BF16) for i in idx]
    n_pow = [_dot(d_b[i], _stack_quarters(d_b[i], consts)) for i in idx]
    t_d = [consts["eye_half"] + diag[i] for i in idx]
    yield None
    terms = 2
    while terms < H:
        sq = [_stack_quarters(n_pow[i].astype(BF16), consts) for i in idx]
        if 2 * terms < H:
            both = [_dot(jnp.concatenate([t_d[i], n_pow[i]], axis=0).astype(BF16), sq[i]) for i in idx]
            t_d = [t_d[i] + both[i][:H] for i in idx]
            n_pow = [both[i][H:] for i in idx]
        else:
            t_d = [t_d[i] + _dot(t_d[i].astype(BF16), sq[i]) for i in idx]
        terms *= 2
        yield None
    t_db = [t_d[i].astype(BF16) for i in idx]
    m1 = [_dot(n21_b[i], _stack_quarters(t_db[i], consts)).astype(BF16) for i in idx]
    yield None
    zero = jnp.zeros_like(m1[0])
    t21 = [_dot(t_db[i], jnp.concatenate([zero, m1[i] * q[0], zero, m1[i] * q[2]], axis=0)) for i in idx]
    t_low = [t_d[i] * second for i in idx]
    yield [jnp.concatenate([t_d[i] - t_low[i], t21[i] + t_low[i]], axis=0) for i in idx]


def _seg_sum(x, blk, n_pairs):
    L = x.shape[0]
    tiles = jnp.concatenate([x[:, i * LANES:(i + 1) * LANES] for i in range(n_pairs)], axis=0).astype(BF16)
    s = _dot(tiles, blk)
    return jnp.concatenate([s[i * L:(i + 1) * L] for i in range(n_pairs)], axis=1)


def _rwkv_chunk(j, c, refs, hand, consts, n_pairs):
    (r_ref, k_ref, v_ref, wa_ref, mr_ref, mk_ref, mv_ref, mwa_ref, mu_ref, w0_ref, a0_ref, lora_ref,
     kk_ref, ka_ref, rk_ref, lnw_ref, lnb_ref) = refs[:17]
    o_ref, s_ref, carry_ref = refs[-3:]
    nm_ref, lhs_ref, vb_ref, arb_ref, bk_ref, pl_ref, bonus_ref, t_ref, lora_scr = hand
    L = CHUNK
    d = n_pairs * LANES
    idx = range(n_pairs)

    is_meta = c == 0
    z = jnp.concatenate([
        jnp.where(is_meta, mr_ref[...], r_ref[j]),
        jnp.where(is_meta, mk_ref[...], k_ref[j]),
        jnp.where(is_meta, mv_ref[...], v_ref[j]),
    ], axis=1).astype(F32)
    row = lax.broadcasted_iota(jnp.int32, z.shape, 0)
    z_prev = jnp.where(row == 0, carry_ref[j, :, :3 * d], pltpu.roll(z, 1, axis=0))
    carry_ref[j, :, :3 * d] = z[L - 1:L, :]
    z = z + (z_prev - z) * mu_ref[:, :3 * d]
    r_all = z[:, :d]
    k_all = z[:, d:2 * d]
    v_all = z[:, 2 * d:3 * d]
    yield

    lora = lora_scr[j]
    lw_all = _sigmoid(w0_ref[...] + lora[:, :d]) * F32(-LOG2_E * math.exp(-0.5))
    a_all = _sigmoid(a0_ref[...] + lora[:, d:])
    hi = lw_all.astype(BF16)
    mid = (lw_all - hi.astype(F32)).astype(BF16)
    cum_all = _dot(consts["tri2"], jnp.concatenate([hi, mid], axis=0))
    yield

    kk = k_all * kk_ref[...]
    kk = kk * lax.rsqrt(jnp.maximum(_seg_sum(kk * kk, consts["ones_blk"], n_pairs), 1e-24))
    k2_all = k_all * (1.0 + (a_all - 1.0) * ka_ref[...])
    av_all = -kk
    bv_all = kk * a_all
    bonus_ref[j] = _seg_sum(r_all * k2_all * rk_ref[...], consts["ones_blk"], n_pairs) * v_all
    yield

    for i in idx:
        sl = slice(i * LANES, (i + 1) * LANES)
        cum = cum_all[:, sl]
        p_incl = jnp.exp2(cum)
        p_excl = jnp.exp2(cum - lw_all[:, sl])
        p_inv = jnp.exp2(-cum)
        a_t = (av_all[:, sl] * p_excl).astype(BF16)
        r_t = (r_all[:, sl] * p_incl).astype(BF16)
        b_t = (bv_all[:, sl] * p_inv).astype(BF16)
        k_t = (k2_all[:, sl] * p_inv).astype(BF16)
        g = _dot_nt(jnp.concatenate([a_t, r_t], axis=0),
                    jnp.concatenate([_stack_heads(b_t, consts), _stack_heads(k_t, consts)], axis=0))
        nm_ref[j, i] = g[:L, :2 * L] * consts["tri_strict"]
        a_ak = (g[:L, 2 * L:] * consts["tri_strict"]).astype(BF16)
        a_rk = (g[L:, 2 * L:] * consts["tri_incl"]).astype(BF16)
        lhs_ref[j, i] = jnp.concatenate([jnp.concatenate([a_t, a_ak], axis=1),
                                         jnp.concatenate([r_t, a_rk], axis=1)], axis=0)
        arb_ref[j, i] = (g[L:, :2 * L] * consts["tri_incl"]).astype(BF16)
        vb_ref[j, i] = v_all[:, sl].astype(BF16)
        bk_ref[j, i] = jnp.concatenate([b_t, k_t], axis=0)
        pl_ref[j, i] = jnp.broadcast_to(p_incl[L - 1:L, :], (8, LANES))
        yield

    t_mats = None
    for t_mats in _unit_lower_inverse([nm_ref[j, i] for i in idx], consts):
        if t_mats is None:
            yield
    for i in idx:
        t_ref[j, i] = t_mats[i].astype(BF16)
    yield

    v_b = [vb_ref[j, i] for i in idx]
    s_prev = [s_ref[j, i] for i in idx]
    xy = [_dot(lhs_ref[j, i], jnp.concatenate([s_prev[i].astype(BF16).T, _stack_heads(v_b[i], consts)], axis=0))
          for i in idx]
    yield
    u_b = [_dot(t_ref[j, i], _stack_heads(xy[i][:L].astype(BF16), consts)).astype(BF16) for i in idx]
    yield
    ys = [xy[i][L:] + _dot(arb_ref[j, i], _stack_heads(u_b[i], consts)) for i in idx]
    for i in idx:
        ds = _dot_tn(jnp.concatenate([u_b[i], v_b[i]], axis=0), bk_ref[j, i])
        s_ref[j, i] = (s_prev[i] + ds * consts["blockdiag"]) * pl_ref[j, i][0:1, :]
    yield

    y_all = jnp.concatenate(ys, axis=1)
    dy = y_all - _seg_sum(y_all, consts["mean_blk"], n_pairs)
    yield
    var = _seg_sum(dy * dy, consts["mean_blk"], n_pairs)
    yn = dy * lax.rsqrt(var + GN_EPS) * lnw_ref[...] + lnb_ref[...]
    o_ref[j] = (yn + bonus_ref[j]).astype(o_ref.dtype)


def _run_staggered(gens, lead):
    done = [False] * len(gens)
    tick = 0
    while not all(done):
        for k, gen in enumerate(gens):
            if not done[k] and tick >= k * lead:
                try:
                    next(gen)
                except StopIteration:
                    done[k] = True
        tick += 1


def _rwkv_kernel(*refs, n_pairs, rows):
    c = pl.program_id(1)
    n_in = 19
    hand = refs[n_in + 1:n_in + 10]
    s_ref, carry_ref = refs[-2:]
    kernel_refs = refs[:n_in + 1] + (s_ref, carry_ref)

    @pl.when(c == 0)
    def _():
        s_ref[...] = jnp.zeros_like(s_ref)
        carry_ref[...] = jnp.zeros_like(carry_ref)

    consts = _rwkv_consts(refs[17], refs[18])

    wa_ref, mwa_ref, mu_ref, lora_ref = refs[3], refs[7], refs[8], refs[11]
    lora_scr = hand[8]
    d = n_pairs * LANES
    L = CHUNK
    row = lax.broadcasted_iota(jnp.int32, (L, LANES), 0)
    lora_in = []
    for j in range(rows):
        wa = jnp.where(c == 0, mwa_ref[...], wa_ref[j]).astype(F32)
        wa_prev = jnp.where(row == 0, carry_ref[j, :, 3 * d:], pltpu.roll(wa, 1, axis=0))
        carry_ref[j, :, 3 * d:] = wa[L - 1:L, :]
        wa = wa + (wa_prev - wa) * mu_ref[:, 3 * d:]
        lora_in.append(jnp.where(consts["low"], jnp.tanh(wa), wa).astype(BF16))
    lora_all = _dot(jnp.concatenate(lora_in, axis=0), lora_ref[...])
    for j in range(rows):
        lora_scr[j] = lora_all[j * L:(j + 1) * L]

    gens =[_rwkv_chunk(j, c, kernel_refs, hand, consts, n_pairs) for j in range(rows)]
    _run_staggered(gens, RWKV_LEAD)


def _rwkv(p_main, p_tail, pm_meta, pt_meta, params, batch, seq):
    d = p_main.shape[1] // 8
    n_pairs = d // LANES
    nc = seq // CHUNK
    rows = RWKV_ROWS if batch % RWKV_ROWS == 0 else 1
    meta_row = BLOCK // CHUNK - 1
    mu, w0, a0, lora_w, k_k, k_a, r_k, ln_w, ln_b = params
    kern = functools.partial(_rwkv_kernel, n_pairs=n_pairs, rows=rows)
    cf, cb = _rwkv_const_arrays()
    pm3 = p_main.reshape(batch, seq, p_main.shape[1])
    pt3 = p_tail.reshape(batch, seq, p_tail.shape[1])

    def main_spec(col):
        return pl.BlockSpec((rows, CHUNK, d), lambda b, c: (b, jnp.maximum(c - 1, 0), col))

    def meta_spec(col):
        return pl.BlockSpec((CHUNK, d), lambda b, c: (meta_row, col))

    def vec_spec(w):
        return pl.BlockSpec((1, w), lambda b, c: (0, 0))

    out = pl.pallas_call(
        kern,
        grid=(batch // rows, nc + 1),
        in_specs=[
            main_spec(2), main_spec(3), main_spec(4),
            pl.BlockSpec((rows, CHUNK, LANES), lambda b, c: (b, jnp.maximum(c - 1, 0), 2)),
            meta_spec(2), meta_spec(3), meta_spec(4),
            pl.BlockSpec((CHUNK, LANES), lambda b, c: (meta_row, 2)),
            vec_spec(3 * d + LANES), vec_spec(d), vec_spec(d),
            pl.BlockSpec((LANES, 2 * d), lambda b, c: (0, 0)),
            vec_spec(d), vec_spec(d), vec_spec(d), vec_spec(d), vec_spec(d),
            pl.BlockSpec(cf.shape, lambda b, c: (0, 0)), pl.BlockSpec(cb.shape, lambda b, c: (0, 0)),
        ],
        out_specs=pl.BlockSpec((rows, CHUNK, d), lambda b, c: (b, jnp.maximum(c - 1, 0), 0)),
        out_shape=jax.ShapeDtypeStruct((batch, seq, d), BF16),
        scratch_shapes=[
            pltpu.VMEM((rows, n_pairs, CHUNK, LANES), F32),
            pltpu.VMEM((rows, n_pairs, 2 * CHUNK, 2 * LANES), BF16),
            pltpu.VMEM((rows, n_pairs, CHUNK, LANES), BF16),
            pltpu.VMEM((rows, n_pairs, CHUNK, LANES), BF16),
            pltpu.VMEM((rows, n_pairs, 2 * CHUNK, LANES), BF16),
            pltpu.VMEM((rows, n_pairs, 8, LANES), F32),
            pltpu.VMEM((rows, CHUNK, d), F32),
            pltpu.VMEM((rows, n_pairs, CHUNK, LANES), BF16),
            pltpu.VMEM((rows, CHUNK, 2 * d), F32),
            pltpu.VMEM((rows, n_pairs, LANES, LANES), F32),
            pltpu.VMEM((rows, 1, 3 * d + LANES), F32),
        ],
        compiler_params=pltpu.CompilerParams(
            dimension_semantics=("parallel", "arbitrary"), vmem_limit_bytes=VMEM_LIMIT),
        name="rwkv7_chunked",
    )(pm3, pm3, pm3, pt3, pm_meta, pm_meta, pm_meta, pt_meta,
      mu, w0, a0, lora_w, k_k, k_a, r_k, ln_w, ln_b, cf, cb)
    return out.reshape(batch * seq, d)


def _merge_kernel(x_ref, att_ref, ga_ref, rw_ref, gr_ref, ma_ref, mr_ref, wa_ref, wr_ref, wo_ref, fw_ref, o_ref):
    def gated(val_ref, gate_ref):
        h = gate_ref[...] * 0.5
        return val_ref[...] * (h * (jnp.tanh(h) + 1.0))

    ya = _dot(gated(att_ref, ga_ref), wa_ref[...])
    yr = _dot(gated(rw_ref, gr_ref), wr_ref[...])
    merged = _sigmoid(ma_ref[...].astype(F32)) * ya + _sigmoid(mr_ref[...].astype(F32)) * yr
    h = x_ref[...] + _dot(merged.astype(BF16), wo_ref[...])
    ms = jnp.mean(h * h, axis=-1, keepdims=True)
    o_ref[...] = h * lax.rsqrt(ms + RMS_EPS) * fw_ref[...]


def _merge(x2d, att, p_main, rw, w_a, w_r, w_o, final_w):
    m, d = x2d.shape
    tm = min(_row_tile(m), 512)

    def rows(col):
        return pl.BlockSpec((tm, d), lambda i: (i, col))

    def full():
        return pl.BlockSpec((d, d), lambda i: (0, 0))

    return pl.pallas_call(
        _merge_kernel,
        grid=(m // tm,),
        in_specs=[rows(0), rows(0), rows(1), rows(0), rows(5), rows(6), rows(7), full(), full(), full(),
                  pl.BlockSpec((1, d), lambda i: (0, 0))],
        out_specs=rows(0),
        out_shape=jax.ShapeDtypeStruct((m, d), F32),
        compiler_params=pltpu.CompilerParams(
            dimension_semantics=("parallel",), vmem_limit_bytes=VMEM_LIMIT),
        name="merge_out",
    )(x2d, att, p_main, rw, p_main, p_main, p_main, w_a, w_r, w_o, final_w)


def _rope_table(pos):
    half = HEAD // 2
    inv = 1.0 / (ROPE_THETA ** (jnp.arange(half, dtype=F32) / half))
    ang = pos[:, None] * inv[None, :]
    cos = jnp.cos(ang)
    sin = jnp.sin(ang)
    cos_pair = jnp.concatenate([cos, cos, cos, cos], axis=1)
    sin_pair = jnp.concatenate([-sin, sin, -sin, sin], axis=1)
    return jnp.concatenate([cos_pair, sin_pair], axis=1)


def kernel(x, meta_tokens, norm_w, w_in, att_sinks, rk_mu, rk_w0, rk_w2, rk_a0, rk_a2, rk_k_k, rk_k_a, rk_r_k,
           rk_ln_w, rk_ln_b, w_branch_att, w_branch_rwkv, w_out, final_norm_w):
    batch, seq, d = x.shape
    assert seq % BLOCK == 0 and d % LANES == 0 and norm_w.shape[0] == 1
    kvw = KV_HEADS * HEAD
    assert kvw == LANES and 2 * LORA == LANES

    w_bf = w_in[0].astype(BF16)
    o = 0
    col = {}
    for name, size in (("q", d), ("ka", kvw), ("va", kvw), ("ga", d), ("r", d), ("kr", d), ("vr", d),
                       ("wa", 2 * LORA), ("gr", d), ("ma", d), ("mr", d)):
        col[name] = o
        o += size
    assert o == w_bf.shape[1] and all(c % LANES == 0 for c in col.values())
    main_cols = tuple(col[n] for n in ("q", "ga", "r", "kr", "vr", "gr", "ma", "mr"))

    nw = norm_w[0][None, :]
    x2d = x.reshape(batch * seq, d)
    meta_blk = jnp.concatenate([jnp.zeros((BLOCK - N_META, d), x.dtype), meta_tokens.astype(x.dtype)], axis=0)

    cs_x = _rope_table(N_META + jnp.arange(seq, dtype=F32))
    cs_meta = _rope_table(jnp.maximum(jnp.arange(BLOCK, dtype=F32) - (BLOCK - N_META), 0.0))
    p_main, p_tail = _inproj(x2d, nw, cs_x, w_bf, main_cols, col["ka"], col["wa"])
    pm_meta, pt_meta = _inproj(meta_blk, nw, cs_meta, w_bf, main_cols, col["ka"], col["wa"])

    att = _attention(att_sinks[0], p_main, p_tail, pt_meta, batch, seq)

    zeros_lora = jnp.zeros((LORA, d), F32)
    lora_w = jnp.concatenate([
        jnp.concatenate([rk_w2[0], zeros_lora], axis=0),
        jnp.concatenate([zeros_lora, rk_a2[0]], axis=0)], axis=1).astype(BF16)
    params = (rk_mu[0][None, :], rk_w0[0][None, :], rk_a0[0][None, :], lora_w, rk_k_k[0][None, :],
              rk_k_a[0][None, :], rk_r_k[0].reshape(1, d), rk_ln_w[0][None, :], rk_ln_b[0][None, :])
    rw = _rwkv(p_main, p_tail, pm_meta, pt_meta, params, batch, seq)

    y = _merge(x2d, att, p_main, rw, w_branch_att[0].astype(BF16), w_branch_rwkv[0].astype(BF16),
               w_out[0].astype(BF16), final_norm_w[None, :])
    return y.reshape(batch, seq, d)
```
